```python
import jax
import jax.numpy as jnp
from jax import lax
import numpy as np

D_MODEL = 2048
BATCH = 16
SEQ = 2048
DEPTH = 1

HEAD_DIM = 64
MIX_WIDTH = D_MODEL
A_WIDTH = MIX_WIDTH // 2
B_WIDTH = MIX_WIDTH - A_WIDTH
A_HEADS = A_WIDTH // HEAD_DIM
B_Q_HEADS = B_WIDTH // HEAD_DIM
B_KV_HEADS = max(1, B_Q_HEADS // 8)
B_GROUP = B_Q_HEADS // B_KV_HEADS
KV_B_WIDTH = B_KV_HEADS * HEAD_DIM
QKV_WIDTH = 3 * A_WIDTH + B_WIDTH + 2 * KV_B_WIDTH
DILATED_PATTERNS = ((128, 1), (512, 4), (2048, 16))
SWA_WINDOW = 128
BLOCK = 128
D_FF = ((8 * D_MODEL // 3 + 255) // 256) * 256
PLE_DIM = 256
ALIBI_MAX_BIAS = 8.0
EPS = 1e-6

kernel_name = 'hybrid_dilated_swa_macaron_layer'


def rmsnorm(x, g):
    xf = x.astype(jnp.float32)
    y = xf * lax.rsqrt(jnp.mean(xf * xf, axis=-1, keepdims=True) + EPS)
    return (y * g.astype(jnp.float32)).astype(x.dtype)


def swiglu(x, w_gate, w_up, w_down):
    return (jax.nn.silu(x @ w_gate) * (x @ w_up)) @ w_down


def alibi_slopes(n):
    return jnp.exp2(-ALIBI_MAX_BIAS * (jnp.arange(n, dtype=jnp.float32) + 1.0) / n)


def banded_attention(q, k, v, slopes, dist_scale, window, sinks=None):
    n, hk, g, length, dh = q.shape
    nb = -(-length // BLOCK)
    pad = nb * BLOCK - length
    qb = jnp.pad(q, ((0, 0), (0, 0), (0, 0), (0, pad), (0, 0))).reshape(n, hk, g, nb, BLOCK, dh)

    def band(t):
        t = jnp.pad(t, ((0, 0), (0, 0), (BLOCK, pad), (0, 0))).reshape(n, hk, nb + 1, BLOCK, dh)
        return jnp.concatenate([t[:, :, :-1], t[:, :, 1:]], axis=3)

    kb, vb = band(k), band(v)
    s = jnp.einsum('nhgbqd,nhbkd->nhgbqk', qb, kb).astype(jnp.float32) * (dh ** -0.5)
    qi = jnp.arange(BLOCK)[:, None]
    kj = jnp.arange(2 * BLOCK)[None, :]
    rel = qi + BLOCK - kj
    blk = jnp.arange(nb)[:, None, None]
    valid = (rel >= 0) & (rel <= window) & ((blk > 0) | (kj >= BLOCK))
    bias = -slopes.astype(jnp.float32)[:, :, None, None, None] * (dist_scale * rel).astype(jnp.float32)
    s = jnp.where(valid, s + bias, -jnp.inf)
    m = jnp.max(s, axis=-1, keepdims=True)
    if sinks is not None:
        sk = sinks.astype(jnp.float32)[:, :, None, None, None]
        m = jnp.maximum(m, sk)
    e = jnp.exp(s - m)
    denom = jnp.sum(e, axis=-1, keepdims=True)
    if sinks is not None:
        denom = denom + jnp.exp(sk - m)
    o = jnp.einsum('nhgbqk,nhbkd->nhgbqd', e, vb.astype(jnp.float32)) / denom
    lse = (m + jnp.log(denom))[..., 0]
    o = o.reshape(n, hk, g, nb * BLOCK, dh)[:, :, :, :length]
    lse = lse.reshape(n, hk, g, nb * BLOCK)[:, :, :, :length]
    return o, lse


def dilated_attention(q, k, v, slopes):
    bsz, seq, nh, dh = q.shape
    outs, lses = [], []
    for window, dil in DILATED_PATTERNS:
        sub = seq // dil

        def fold(t):
            return t.reshape(bsz, sub, dil, nh, dh).transpose(0, 2, 3, 1, 4).reshape(bsz * dil, nh, sub, dh)

        o, lse = banded_attention(fold(q)[:, :, None], fold(k), fold(v), slopes[:, None], dil, window // dil)
        outs.append(o[:, :, 0].reshape(bsz, dil, nh, sub, dh).transpose(0, 3, 1, 2, 4).reshape(bsz, seq, nh, dh))
        lses.append(lse[:, :, 0].reshape(bsz, dil, nh, sub).transpose(0, 3, 1, 2).reshape(bsz, seq, nh))
    w = jax.nn.softmax(jnp.stack(lses, axis=0), axis=0)
    out = jnp.sum(w[..., None] * jnp.stack(outs, axis=0), axis=0)
    return out.astype(q.dtype)


def sink_swa_gqa(q, k, v, slopes, sinks):
    bsz, seq, _ = q.shape
    qh = q.reshape(bsz, seq, B_KV_HEADS, B_GROUP, HEAD_DIM).transpose(0, 2, 3, 1, 4)
    kh = k.reshape(bsz, seq, B_KV_HEADS, HEAD_DIM).transpose(0, 2, 1, 3)
    vh = v.reshape(bsz, seq, B_KV_HEADS, HEAD_DIM).transpose(0, 2, 1, 3)
    o, _ = banded_attention(qh, kh, vh, slopes, 1, SWA_WINDOW - 1, sinks)
    return o.transpose(0, 3, 1, 2, 4).reshape(bsz, seq, B_WIDTH).astype(q.dtype)


def setup_inputs(seed: int = 0) -> dict:
    key = jax.random.key(seed)
    keys = jax.random.split(key, 32)
    ctr = [0]

    def nk():
        ctr[0] += 1
        return keys[ctr[0] - 1]

    def w(shape, fan_in):
        return jax.random.normal(nk(), (DEPTH,) + shape, jnp.float32) * (fan_in ** -0.5)

    def gain(n):
        return 1.0 + 0.05 * jax.random.normal(nk(), (DEPTH, n), jnp.float32)

    def small(shape, scale):
        return scale * jax.random.normal(nk(), (DEPTH,) + shape, jnp.float32)

    return {
        'x': jax.random.normal(nk(), (BATCH, SEQ, D_MODEL), jnp.float32),
        'p': jax.random.normal(nk(), (DEPTH, BATCH, SEQ, PLE_DIM), jnp.float32),
        'g_ffn1_pre': gain(D_MODEL),
        'w_ffn1_gate': w((D_MODEL, D_FF), D_MODEL),
        'w_ffn1_up': w((D_MODEL, D_FF), D_MODEL),
        'w_ffn1_down': w((D_FF, D_MODEL), D_FF),
        'g_ffn1_post': gain(D_MODEL),
        'g_mix_pre': gain(D_MODEL),
        'w_qkv': w((D_MODEL, QKV_WIDTH), D_MODEL),
        'b_qkv': small((QKV_WIDTH,), 0.02),
        'attn_sinks': small((B_KV_HEADS, B_GROUP), 1.0),
        'g_out_a': gain(A_WIDTH),
        'g_out_b': gain(B_WIDTH),
        'w_o': w((MIX_WIDTH, D_MODEL), MIX_WIDTH),
        'b_o': small((D_MODEL,), 0.02),
        'g_mix_post': gain(D_MODEL),
        'g_ffn2_pre': gain(D_MODEL),
        'w_ffn2_gate': w((D_MODEL, D_FF), D_MODEL),
        'w_ffn2_up': w((D_MODEL, D_FF), D_MODEL),
        'w_ffn2_down': w((D_FF, D_MODEL), D_FF),
        'g_ffn2_post': gain(D_MODEL),
        'g_ple_pre': gain(D_MODEL),
        'w_ple_gate': w((D_MODEL, D_MODEL), D_MODEL),
        'w_ple_proj': w((PLE_DIM, D_MODEL), PLE_DIM),
        'g_ple_post': gain(D_MODEL),
    }


def reference(x, p, g_ffn1_pre, w_ffn1_gate, w_ffn1_up, w_ffn1_down, g_ffn1_post,
              g_mix_pre, w_qkv, b_qkv, attn_sinks, g_out_a, g_out_b, w_o, b_o, g_mix_post,
              g_ffn2_pre, w_ffn2_gate, w_ffn2_up, w_ffn2_down, g_ffn2_post,
              g_ple_pre, w_ple_gate, w_ple_proj, g_ple_post):
    bsz, seq, _ = x.shape
    slopes = alibi_slopes(A_HEADS + B_Q_HEADS)
    slopes_a = slopes[0::2]
    slopes_b = slopes[1::2].reshape(B_KV_HEADS, B_GROUP)
    splits = [A_WIDTH, 2 * A_WIDTH, 3 * A_WIDTH, 3 * A_WIDTH + B_WIDTH, 3 * A_WIDTH + B_WIDTH + KV_B_WIDTH]
    h = x
    for i in range(DEPTH):
        f = swiglu(rmsnorm(h, g_ffn1_pre[i]), w_ffn1_gate[i], w_ffn1_up[i], w_ffn1_down[i])
        h = h + 0.5 * rmsnorm(f, g_ffn1_post[i])
        u = rmsnorm(h, g_mix_pre[i])
        z = u @ w_qkv[i] + b_qkv[i]
        qa, ka, va, qb, kb, vb = jnp.split(z, splits, axis=-1)
        qa = qa.reshape(bsz, seq, A_HEADS, HEAD_DIM)
        ka = ka.reshape(bsz, seq, A_HEADS, HEAD_DIM)
        va = va.reshape(bsz, seq, A_HEADS, HEAD_DIM)
        oa = dilated_attention(qa, ka, va, slopes_a).reshape(bsz, seq, A_WIDTH)
        ob = sink_swa_gqa(qb, kb, vb, slopes_b, attn_sinks[i])
        o = jnp.concatenate([rmsnorm(oa, g_out_a[i]), rmsnorm(ob, g_out_b[i])], axis=-1)
        o = o @ w_o[i] + b_o[i]
        h = h + rmsnorm(o, g_mix_post[i])
        f = swiglu(rmsnorm(h, g_ffn2_pre[i]), w_ffn2_gate[i], w_ffn2_up[i], w_ffn2_down[i])
        h = h + 0.5 * rmsnorm(f, g_ffn2_post[i])
        gate = jax.nn.sigmoid(rmsnorm(h, g_ple_pre[i]) @ w_ple_gate[i])
        e = p[i] @ w_ple_proj[i]
        h = h + rmsnorm(gate * e, g_ple_post[i])
    return h
```

```python
import functools

import jax
import jax.numpy as jnp
from jax import lax
from jax.experimental import pallas as pl
from jax.experimental.pallas import tpu as pltpu

F32 = jnp.float32
BF16 = jnp.bfloat16

HEAD_DIM = 64
LANES = 128
BLOCK = 128
A_HEADS = 16
B_Q_HEADS = 16
B_KV_HEADS = 2
B_GROUP = B_Q_HEADS // B_KV_HEADS
A_WIDTH = A_HEADS * HEAD_DIM
B_WIDTH = B_Q_HEADS * HEAD_DIM
KV_B_WIDTH = B_KV_HEADS * HEAD_DIM
ZB_WIDTH = B_WIDTH + 2 * KV_B_WIDTH
DILATIONS = (1, 4, 16)
A_WINDOW = 128
B_WINDOW = 127
ALIBI_MAX_BIAS = 8.0
EPS = 1e-6
QK_SCALE = HEAD_DIM ** -0.5
NEG_INF = float("-inf")

VMEM_LIMIT = 56 * 1024 * 1024

FFN_TM = 512
FFN_TF = 512
QKV_TM = 512
OUT_TM = 512
PLE_TM = 512


def _rms(x, g):
    return x * lax.rsqrt(jnp.mean(x * x, axis=-1, keepdims=True) + EPS) * g


def _dot(a, b):
    return jnp.dot(a, b, preferred_element_type=F32)


def _dot_nt(a, b):
    return lax.dot_general(a, b, (((1,), (1,)), ((), ())), preferred_element_type=F32)


def _params(*semantics):
    return pltpu.CompilerParams(dimension_semantics=semantics, vmem_limit_bytes=VMEM_LIMIT)


def _ffn_kernel(x_ref, gpre_ref, wg_ref, wu_ref, wd_ref, gpost_ref, o_ref, xn_ref, acc_ref):
    j = pl.program_id(1)

    @pl.when(j == 0)
    def _():
        xn_ref[...] = _rms(x_ref[...], gpre_ref[...]).astype(BF16)
        acc_ref[...] = jnp.zeros_like(acc_ref)

    xn = xn_ref[...]
    gate = _dot(xn, wg_ref[...])
    up = _dot(xn, wu_ref[...])
    act = (gate * jax.nn.sigmoid(gate) * up).astype(BF16)
    acc_ref[...] += _dot(act, wd_ref[...])

    @pl.when(j == pl.num_programs(1) - 1)
    def _():
        o_ref[...] = x_ref[...] + 0.5 * _rms(acc_ref[...], gpost_ref[...])


def _ffn(h, g_pre, w_gate, w_up, w_down, g_post):
    t, d = h.shape
    d_ff = w_gate.shape[1]
    tm, tf = FFN_TM, FFN_TF
    return pl.pallas_call(
        _ffn_kernel,
        grid=(t // tm, d_ff // tf),
        in_specs=[
            pl.BlockSpec((tm, d), lambda i, j: (i, 0)),
            pl.BlockSpec((1, d), lambda i, j: (0, 0)),
            pl.BlockSpec((d, tf), lambda i, j: (0, j)),
            pl.BlockSpec((d, tf), lambda i, j: (0, j)),
            pl.BlockSpec((tf, d), lambda i, j: (j, 0)),
            pl.BlockSpec((1, d), lambda i, j: (0, 0)),
        ],
        out_specs=pl.BlockSpec((tm, d), lambda i, j: (i, 0)),
        out_shape=jax.ShapeDtypeStruct((t, d), F32),
        scratch_shapes=[pltpu.VMEM((tm, d), BF16), pltpu.VMEM((tm, d), F32)],
        compiler_params=_params("parallel", "arbitrary"),
        name="ffn",
    )(h, g_pre, w_gate, w_up, w_down, g_post)


def _qkv_kernel(h_ref, g_ref, wa_ref, ba_ref, wb_ref, bb_ref,
                nat_ref, a4_ref, a16_ref, zb_ref, xn_ref, zs_ref):
    j = pl.program_id(1)
    tm = h_ref.shape[0]

    @pl.when(j == 0)
    def _():
        xn = _rms(h_ref[...], g_ref[...]).astype(BF16)
        xn_ref[...] = xn
        zb_ref[...] = (_dot(xn, wb_ref[...]) + bb_ref[...]).astype(BF16)

    z = _dot(xn_ref[...], wa_ref[...]) + ba_ref[...]
    nat_ref[...] = z.astype(BF16)
    for c in range(zs_ref.shape[0]):
        cols = slice(c * LANES, (c + 1) * LANES)
        zs_ref[c] = z[:, cols]
        for r in range(4):
            a4_ref[0, r, :, cols] = zs_ref[c, pl.ds(r, tm // 4, stride=4), :].astype(BF16)
        for r in range(16):
            a16_ref[0, r, :, cols] = zs_ref[c, pl.ds(r, tm // 16, stride=16), :].astype(BF16)


def _qkv(h, g, w_a, b_a, w_b, b_b, bsz, seq):
    t, d = h.shape
    tm = QKV_TM
    tn = A_WIDTH
    per_seq = seq // tm
    return pl.pallas_call(
        _qkv_kernel,
        grid=(t // tm, 3),
        in_specs=[
            pl.BlockSpec((tm, d), lambda i, j: (i, 0)),
            pl.BlockSpec((1, d), lambda i, j: (0, 0)),
            pl.BlockSpec((d, tn), lambda i, j: (0, j)),
            pl.BlockSpec((1, tn), lambda i, j: (0, j)),
            pl.BlockSpec((d, ZB_WIDTH), lambda i, j: (0, 0)),
            pl.BlockSpec((1, ZB_WIDTH), lambda i, j: (0, 0)),
        ],
        out_specs=[
            pl.BlockSpec((tm, tn), lambda i, j: (i, j)),
            pl.BlockSpec((1, 4, tm // 4, tn), lambda i, j: (i // per_seq, 0, i % per_seq, j)),
            pl.BlockSpec((1, 16, tm // 16, tn), lambda i, j: (i // per_seq, 0, i % per_seq, j)),
            pl.BlockSpec((tm, ZB_WIDTH), lambda i, j: (i, 0)),
        ],
        out_shape=[
            jax.ShapeDtypeStruct((t, 3 * A_WIDTH), BF16),
            jax.ShapeDtypeStruct((bsz, 4, seq // 4, 3 * A_WIDTH), BF16),
            jax.ShapeDtypeStruct((bsz, 16, seq // 16, 3 * A_WIDTH), BF16),
            jax.ShapeDtypeStruct((t, ZB_WIDTH), BF16),
        ],
        scratch_shapes=[pltpu.VMEM((tm, d), BF16), pltpu.VMEM((tn // LANES, tm, LANES), F32)],
        compiler_params=_params("parallel", "arbitrary"),
        name="qkv",
    )(h, g, w_a, b_a, w_b, b_b)


def _band_consts(nk, window):
    qi = lax.broadcasted_iota(jnp.int32, (BLOCK, nk), 0)
    kj = lax.broadcasted_iota(jnp.int32, (BLOCK, nk), 1)
    rel = qi + (nk - BLOCK) - kj
    valid = (rel >= 0) & (rel <= window)
    return rel.astype(F32), valid


def _softmax_parts(s, rel_f, valid, neg_slope_dist):
    s = jnp.where(valid, s * QK_SCALE + rel_f * neg_slope_dist, NEG_INF)
    return s, jnp.max(s, axis=-1, keepdims=True)


def _attn_a_kernel(slopes_ref,
                   q1_ref, k1_ref, v1_ref, q4_ref, k4_ref, v4_ref, q16_ref, k16_ref, v16_ref,
                   o_ref, acc_ref, m_ref, l_ref):
    hp = pl.program_id(1)
    seq = q1_ref.shape[1]
    lane = lax.broadcasted_iota(jnp.int32, (1, LANES), 1)
    head0 = lane < HEAD_DIM
    neg_slopes = (-slopes_ref[4 * hp], -slopes_ref[4 * hp + 2])
    rel2, valid2 = _band_consts(2 * BLOCK, A_WINDOW)
    rel1, valid1 = _band_consts(BLOCK, A_WINDOW)

    def unit(q, k, v, dil, rel_f, valid):
        parts = []
        for hh, keep in enumerate((head0, ~head0)):
            qm = jnp.where(keep, q, jnp.zeros_like(q))
            s, m = _softmax_parts(_dot_nt(qm, k), rel_f, valid, neg_slopes[hh] * float(dil))
            e = jnp.exp(s - m)
            l = jnp.sum(e, axis=-1, keepdims=True)
            parts.append((_dot(e.astype(BF16), v), m, l))
        (pv0, m0, l0), (pv1, m1, l1) = parts
        return (jnp.where(head0, pv0, pv1), jnp.where(head0, m0, m1), jnp.where(head0, l0, l1))

    def store(branch, rows, res):
        acc_ref[branch, rows, :] = res[0]
        m_ref[branch, rows, :] = res[1]
        l_ref[branch, rows, :] = res[2]

    store(0, pl.ds(0, BLOCK),
          unit(q1_ref[0, pl.ds(0, BLOCK), :], k1_ref[0, pl.ds(0, BLOCK), :],
               v1_ref[0, pl.ds(0, BLOCK), :], 1, rel1, valid1))

    def body1(blk, carry):
        q0 = pl.multiple_of(blk * BLOCK, BLOCK)
        k0 = pl.multiple_of((blk - 1) * BLOCK, BLOCK)
        store(0, pl.ds(q0, BLOCK),
              unit(q1_ref[0, pl.ds(q0, BLOCK), :], k1_ref[0, pl.ds(k0, 2 * BLOCK), :],
                   v1_ref[0, pl.ds(k0, 2 * BLOCK), :], 1, rel2, valid2))
        return carry

    lax.fori_loop(1, seq // BLOCK, body1, 0)

    nb4 = seq // 4 // BLOCK

    def body4(r, carry):
        store(1, pl.ds(r, BLOCK, stride=4),
              unit(q4_ref[0, r, pl.ds(0, BLOCK), :], k4_ref[0, r, pl.ds(0, BLOCK), :],
                   v4_ref[0, r, pl.ds(0, BLOCK), :], 4, rel1, valid1))
        for blk in range(1, nb4):
            store(1, pl.ds(r + 4 * blk * BLOCK, BLOCK, stride=4),
                  unit(q4_ref[0, r, pl.ds(blk * BLOCK, BLOCK), :],
                       k4_ref[0, r, pl.ds((blk - 1) * BLOCK, 2 * BLOCK), :],
                       v4_ref[0, r, pl.ds((blk - 1) * BLOCK, 2 * BLOCK), :], 4, rel2, valid2))
        return carry

    lax.fori_loop(0, 4, body4, 0)

    def body16(r, carry):
        store(2, pl.ds(r, BLOCK, stride=16),
              unit(q16_ref[0, r], k16_ref[0, r], v16_ref[0, r], 16, rel1, valid1))
        return carry

    lax.fori_loop(0, 16, body16, 0)

    def merge(blk, carry):
        rows = pl.ds(pl.multiple_of(blk * BLOCK, BLOCK), BLOCK)
        m0, m1, m2 = m_ref[0, rows, :], m_ref[1, rows, :], m_ref[2, rows, :]
        mx = jnp.maximum(jnp.maximum(m0, m1), m2)
        w0, w1, w2 = jnp.exp(m0 - mx), jnp.exp(m1 - mx), jnp.exp(m2 - mx)
        num = w0 * acc_ref[0, rows, :] + w1 * acc_ref[1, rows, :] + w2 * acc_ref[2, rows, :]
        den = w0 * l_ref[0, rows, :] + w1 * l_ref[1, rows, :] + w2 * l_ref[2, rows, :]
        o_ref[0, rows, :] = num / den
        return carry

    lax.fori_loop(0, seq // BLOCK, merge, 0)


def _attn_a(slopes, nat, a4, a16):
    bsz, seq, _ = nat.shape
    pairs = A_WIDTH // LANES

    def col(base):
        return lambda b, hp: (b, 0, base + hp)

    def col4(base):
        return lambda b, hp: (b, 0, 0, base + hp)

    nat_spec = [pl.BlockSpec((1, seq, LANES), col(i * pairs)) for i in range(3)]
    a4_spec = [pl.BlockSpec((1, 4, seq // 4, LANES), col4(i * pairs)) for i in range(3)]
    a16_spec = [pl.BlockSpec((1, 16, seq // 16, LANES), col4(i * pairs)) for i in range(3)]
    return pl.pallas_call(
        _attn_a_kernel,
        grid=(bsz, pairs),
        in_specs=[pl.BlockSpec(memory_space=pltpu.SMEM)] + nat_spec + a4_spec + a16_spec,
        out_specs=pl.BlockSpec((1, seq, LANES), lambda b, hp: (b, 0, hp)),
        out_shape=jax.ShapeDtypeStruct((bsz, seq, A_WIDTH), F32),
        scratch_shapes=[pltpu.VMEM((3, seq, LANES), F32)] * 3,
        compiler_params=_params("parallel", "parallel"),
        name="attn_a",
    )(slopes, nat, nat, nat, a4, a4, a4, a16, a16, a16)


def _attn_b_kernel(slopes_ref, sinks_ref, q_ref, k_ref, v_ref, o_ref, kd_ref, vd_ref):
    g = pl.program_id(1)
    seq = q_ref.shape[1]
    lane = lax.broadcasted_iota(jnp.int32, (1, LANES), 1)
    head0 = lane < HEAD_DIM
    mine = (lane >= HEAD_DIM).astype(jnp.int32) == g

    def both_halves(ref):
        x = ref[0].astype(F32)
        return jnp.where(mine, x, pltpu.roll(x, HEAD_DIM, 1)).astype(BF16)

    kd_ref[...] = both_halves(k_ref)
    vd_ref[...] = both_halves(v_ref)

    neg_slopes = [-slopes_ref[2 * (g * B_GROUP + j) + 1] for j in range(B_GROUP)]
    sinks = [sinks_ref[g * B_GROUP + j] for j in range(B_GROUP)]
    rel2, valid2 = _band_consts(2 * BLOCK, B_WINDOW)
    rel1, valid1 = _band_consts(BLOCK, B_WINDOW)

    def unit(q0, k, v, rel_f, valid):
        q = q_ref[0, pl.ds(q0, BLOCK), :]
        stacked = []
        for j in range(B_GROUP):
            qp = q[:, (j // 2) * LANES:(j // 2 + 1) * LANES]
            stacked.append(jnp.where(head0 if j % 2 == 0 else ~head0, qp, jnp.zeros_like(qp)))
        s_all = _dot_nt(jnp.concatenate(stacked, axis=0), k)
        probs, denoms = [], []
        for j in range(B_GROUP):
            s, m = _softmax_parts(s_all[j * BLOCK:(j + 1) * BLOCK], rel_f, valid, neg_slopes[j])
            m = jnp.maximum(m, sinks[j])
            e = jnp.exp(s - m)
            denoms.append(jnp.sum(e, axis=-1, keepdims=True) + jnp.exp(sinks[j] - m))
            probs.append(e.astype(BF16))
        pv = _dot(jnp.concatenate(probs, axis=0), v)
        for p in range(B_GROUP // 2):
            even = pv[(2 * p) * BLOCK:(2 * p + 1) * BLOCK] / denoms[2 * p]
            odd = pv[(2 * p + 1) * BLOCK:(2 * p + 2) * BLOCK] / denoms[2 * p + 1]
            o_ref[0, pl.ds(q0, BLOCK), p * LANES:(p + 1) * LANES] = jnp.where(head0, even, odd)

    unit(0, kd_ref[pl.ds(0, BLOCK), :], vd_ref[pl.ds(0, BLOCK), :], rel1, valid1)

    def body(blk, carry):
        q0 = pl.multiple_of(blk * BLOCK, BLOCK)
        k0 = pl.multiple_of((blk - 1) * BLOCK, BLOCK)
        unit(q0, kd_ref[pl.ds(k0, 2 * BLOCK), :], vd_ref[pl.ds(k0, 2 * BLOCK), :], rel2, valid2)
        return carry

    lax.fori_loop(1, seq // BLOCK, body, 0)


def _attn_b(slopes, sinks, zb):
    bsz, seq, _ = zb.shape
    qw = B_GROUP * HEAD_DIM
    k_blk = B_WIDTH // LANES
    return pl.pallas_call(
        _attn_b_kernel,
        grid=(bsz, B_KV_HEADS),
        in_specs=[
            pl.BlockSpec(memory_space=pltpu.SMEM),
            pl.BlockSpec(memory_space=pltpu.SMEM),
            pl.BlockSpec((1, seq, qw), lambda b, g: (b, 0, g)),
            pl.BlockSpec((1, seq, LANES), lambda b, g: (b, 0, k_blk)),
            pl.BlockSpec((1, seq, LANES), lambda b, g: (b, 0, k_blk + 1)),
        ],
        out_specs=pl.BlockSpec((1, seq, qw), lambda b, g: (b, 0, g)),
        out_shape=jax.ShapeDtypeStruct((bsz, seq, B_WIDTH), F32),
        scratch_shapes=[pltpu.VMEM((seq, LANES), BF16)] * 2,
        compiler_params=_params("parallel", "parallel"),
        name="attn_b",
    )(slopes, sinks, zb, zb, zb)


def _out_kernel(h_ref, oa_ref, ob_ref, ga_ref, gb_ref, wo_ref, bo_ref, gpost_ref, o_ref):
    na = _rms(oa_ref[...], ga_ref[...]).astype(BF16)
    nb = _rms(ob_ref[...], gb_ref[...]).astype(BF16)
    o = _dot(na, wo_ref[0:A_WIDTH, :]) + _dot(nb, wo_ref[A_WIDTH:A_WIDTH + B_WIDTH, :]) + bo_ref[...]
    o_ref[...] = h_ref[...] + _rms(o, gpost_ref[...])


def _out_proj(h, oa, ob, g_a, g_b, w_o, b_o, g_post):
    t, d = h.shape
    tm = OUT_TM
    row = lambda i: (i, 0)
    fixed = lambda i: (0, 0)
    return pl.pallas_call(
        _out_kernel,
        grid=(t // tm,),
        in_specs=[
            pl.BlockSpec((tm, d), row),
            pl.BlockSpec((tm, A_WIDTH), row),
            pl.BlockSpec((tm, B_WIDTH), row),
            pl.BlockSpec((1, A_WIDTH), fixed),
            pl.BlockSpec((1, B_WIDTH), fixed),
            pl.BlockSpec((A_WIDTH + B_WIDTH, d), fixed),
            pl.BlockSpec((1, d), fixed),
            pl.BlockSpec((1, d), fixed),
        ],
        out_specs=pl.BlockSpec((tm, d), row),
        out_shape=jax.ShapeDtypeStruct((t, d), F32),
        compiler_params=_params("parallel"),
        name="out_proj",
    )(h, oa, ob, g_a, g_b, w_o, b_o, g_post)


def _ple_kernel(h_ref, p_ref, gpre_ref, wg_ref, wp_ref, gpost_ref, o_ref):
    h = h_ref[...]
    gate = jax.nn.sigmoid(_dot(_rms(h, gpre_ref[...]).astype(BF16), wg_ref[...]))
    e = _dot(p_ref[...].astype(BF16), wp_ref[...])
    o_ref[...] = h + _rms(gate * e, gpost_ref[...])


def _ple(h, p, g_pre, w_gate, w_proj, g_post):
    t, d = h.shape
    pd = p.shape[1]
    tm = PLE_TM
    row = lambda i: (i, 0)
    fixed = lambda i: (0, 0)
    return pl.pallas_call(
        _ple_kernel,
        grid=(t // tm,),
        in_specs=[
            pl.BlockSpec((tm, d), row),
            pl.BlockSpec((tm, pd), row),
            pl.BlockSpec((1, d), fixed),
            pl.BlockSpec((d, d), fixed),
            pl.BlockSpec((pd, d), fixed),
            pl.BlockSpec((1, d), fixed),
        ],
        out_specs=pl.BlockSpec((tm, d), row),
        out_shape=jax.ShapeDtypeStruct((t, d), F32),
        compiler_params=_params("parallel"),
        name="ple",
    )(h, p, g_pre, w_gate, w_proj, g_post)


def _alibi_slopes(n):
    return jnp.exp2(-ALIBI_MAX_BIAS * (jnp.arange(n, dtype=F32) + 1.0) / n)


def kernel(x, p, g_ffn1_pre, w_ffn1_gate, w_ffn1_up, w_ffn1_down, g_ffn1_post, g_mix_pre, w_qkv, b_qkv, attn_sinks, g_out_a, g_out_b, w_o, b_o, g_mix_post, g_ffn2_pre, w_ffn2_gate, w_ffn2_up, w_ffn2_down, g_ffn2_post, g_ple_pre, w_ple_gate, w_ple_proj, g_ple_post):
    bsz, seq, d = x.shape
    depth = p.shape[0]
    t = bsz * seq
    slopes = _alibi_slopes(A_HEADS + B_Q_HEADS)
    bf = lambda w: w.astype(BF16)
    h = x.reshape(t, d)
    for i in range(depth):
        h = _ffn(h, g_ffn1_pre[i][None], bf(w_ffn1_gate[i]), bf(w_ffn1_up[i]), bf(w_ffn1_down[i]),
                 g_ffn1_post[i][None])
        wq = bf(w_qkv[i])
        nat, a4, a16, zb = _qkv(h, g_mix_pre[i][None], wq[:, :3 * A_WIDTH], b_qkv[i][None, :3 * A_WIDTH],
                                wq[:, 3 * A_WIDTH:], b_qkv[i][None, 3 * A_WIDTH:], bsz, seq)
        oa = _attn_a(slopes, nat.reshape(bsz, seq, 3 * A_WIDTH), a4, a16)
        ob = _attn_b(slopes, attn_sinks[i].reshape(-1), zb.reshape(bsz, seq, ZB_WIDTH))
        h = _out_proj(h, oa.reshape(t, A_WIDTH), ob.reshape(t, B_WIDTH), g_out_a[i][None], g_out_b[i][None],
                      bf(w_o[i]), b_o[i][None], g_mix_post[i][None])
        h = _ffn(h, g_ffn2_pre[i][None], bf(w_ffn2_gate[i]), bf(w_ffn2_up[i]), bf(w_ffn2_down[i]),
                 g_ffn2_post[i][None])
        h = _ple(h, p[i].reshape(t, -1), g_ple_pre[i][None], bf(w_ple_gate[i]), bf(w_ple_proj[i]),
                 g_ple_post[i][None])
    return h.reshape(bsz, seq, d)
```

```python
import functools

import jax
import jax.numpy as jnp
from jax import lax
from jax.experimental import pallas as pl
from jax.experimental.pallas import tpu as pltpu

F32 = jnp.float32
BF16 = jnp.bfloat16

HEAD_DIM = 64
LANES = 128
MXU_WIDTH = 256
BLOCK = 128
A_HEADS = 16
B_Q_HEADS = 16
B_KV_HEADS = 2
B_GROUP = B_Q_HEADS // B_KV_HEADS
A_WIDTH = A_HEADS * HEAD_DIM
B_WIDTH = B_Q_HEADS * HEAD_DIM
KV_B_WIDTH = B_KV_HEADS * HEAD_DIM
ZB_WIDTH = B_WIDTH + 2 * KV_B_WIDTH
DILATIONS = (1, 4, 16)
A_WINDOW = 128
B_WINDOW = 127
ALIBI_MAX_BIAS = 8.0
EPS = 1e-6
QK_SCALE = HEAD_DIM ** -0.5
NEG_INF = float("-inf")

VMEM_LIMIT = 61 * 1024 * 1024

FFN_TM = 1024
FFN_TF = 256
QKV_TM = 512
OUT_TM = 512
PLE_TM = 512


def _rms(x, g):
    return x * lax.rsqrt(jnp.mean(x * x, axis=-1, keepdims=True) + EPS) * g


def _dot(a, b):
    return jnp.dot(a, b, preferred_element_type=F32)


def _dot_nt(a, b):
    return lax.dot_general(a, b, (((1,), (1,)), ((), ())), preferred_element_type=F32)


def _resident(shape):
    return pl.BlockSpec(shape, lambda *_: (0,) * len(shape), pipeline_mode=pl.Buffered(1))


def _params(*semantics):
    return pltpu.CompilerParams(dimension_semantics=semantics, vmem_limit_bytes=VMEM_LIMIT)


def _ffn_kernel(x_ref, gpre_ref, wg_ref, wu_ref, wd_ref, gpost_ref, o_ref, xn_ref):
    j = pl.program_id(1)

    @pl.when(j == 0)
    def _():
        xn_ref[...] = _rms(x_ref[...], gpre_ref[...]).astype(BF16)
        o_ref[...] = jnp.zeros_like(o_ref)

    xn = xn_ref[...]
    gate = _dot(xn, wg_ref[...])
    up = _dot(xn, wu_ref[...])
    act = (gate * jax.nn.sigmoid(gate) * up).astype(BF16)
    o_ref[...] += _dot(act, wd_ref[...])

    @pl.when(j == pl.num_programs(1) - 1)
    def _():
        o_ref[...] = x_ref[...] + 0.5 * _rms(o_ref[...], gpost_ref[...])


def _ffn(h, g_pre, w_gate, w_up, w_down, g_post):
    t, d = h.shape
    d_ff = w_gate.shape[1]
    tm, tf = FFN_TM, FFN_TF
    return pl.pallas_call(
        _ffn_kernel,
        grid=(t // tm, d_ff // tf),
        in_specs=[
            pl.BlockSpec((tm, d), lambda i, j: (i, 0)),
            _resident((1, d)),
            pl.BlockSpec((d, tf), lambda i, j: (0, j)),
            pl.BlockSpec((d, tf), lambda i, j: (0, j)),
            pl.BlockSpec((tf, d), lambda i, j: (j, 0)),
            _resident((1, d)),
        ],
        out_specs=pl.BlockSpec((tm, d), lambda i, j: (i, 0)),
        out_shape=jax.ShapeDtypeStruct((t, d), F32),
        scratch_shapes=[pltpu.VMEM((tm, d), BF16)],
        compiler_params=_params("parallel", "arbitrary"),
        name="ffn",
    )(h, g_pre, w_gate, w_up, w_down, g_post)


def _qkv_kernel(h_ref, g_ref, wa_ref, ba_ref, wb_ref, bb_ref,
                nat_ref, a4_ref, a16_ref, zb_ref, zs_ref):
    tm = h_ref.shape[0]
    xn = _rms(h_ref[...], g_ref[...]).astype(BF16)
    zb_ref[...] = (_dot(xn, wb_ref[...]) + bb_ref[...]).astype(BF16)

    for c2 in range(wa_ref.shape[1] // MXU_WIDTH):
        wide = slice(c2 * MXU_WIDTH, (c2 + 1) * MXU_WIDTH)
        z2 = _dot(xn, wa_ref[:, wide]) + ba_ref[:, wide]
        nat_ref[:, wide] = z2.astype(BF16)
        for half in range(MXU_WIDTH // LANES):
            c = c2 * (MXU_WIDTH // LANES) + half
            cols = slice(c * LANES, (c + 1) * LANES)
            zs_ref[c] = z2[:, half * LANES:(half + 1) * LANES]
            for r in range(4):
                a4_ref[0, r, :, cols] = zs_ref[c, pl.ds(r, tm // 4, stride=4), :].astype(BF16)
            for r in range(16):
                a16_ref[0, r, :, cols] = zs_ref[c, pl.ds(r, tm // 16, stride=16), :].astype(BF16)


def _qkv(h, g, w_a, b_a, w_b, b_b, bsz, seq):
    t, d = h.shape
    tm = QKV_TM
    tn = 3 * A_WIDTH
    per_seq = seq // tm
    return pl.pallas_call(
        _qkv_kernel,
        grid=(t // tm,),
        in_specs=[
            pl.BlockSpec((tm, d), lambda i: (i, 0)),
            _resident((1, d)),
            _resident((d, tn)),
            _resident((1, tn)),
            _resident((d, ZB_WIDTH)),
            _resident((1, ZB_WIDTH)),
        ],
        out_specs=[
            pl.BlockSpec((tm, tn), lambda i: (i, 0)),
            pl.BlockSpec((1, 4, tm // 4, tn), lambda i: (i // per_seq, 0, i % per_seq, 0)),
            pl.BlockSpec((1, 16, tm // 16, tn), lambda i: (i // per_seq, 0, i % per_seq, 0)),
            pl.BlockSpec((tm, ZB_WIDTH), lambda i: (i, 0)),
        ],
        out_shape=[
            jax.ShapeDtypeStruct((t, 3 * A_WIDTH), BF16),
            jax.ShapeDtypeStruct((bsz, 4, seq // 4, 3 * A_WIDTH), BF16),
            jax.ShapeDtypeStruct((bsz, 16, seq // 16, 3 * A_WIDTH), BF16),
            jax.ShapeDtypeStruct((t, ZB_WIDTH), BF16),
        ],
        scratch_shapes=[pltpu.VMEM((tn // LANES, tm, LANES), F32)],
        compiler_params=_params("parallel"),
        name="qkv",
    )(h, g, w_a, b_a, w_b, b_b)


def _band_consts(nk, window):
    qi = lax.broadcasted_iota(jnp.int32, (BLOCK, nk), 0)
    kj = lax.broadcasted_iota(jnp.int32, (BLOCK, nk), 1)
    rel = qi + (nk - BLOCK) - kj
    valid = (rel >= 0) & (rel <= window)
    return rel.astype(F32), valid


def _softmax_parts(s, rel_f, valid, neg_slope_dist):
    s = jnp.where(valid, s * QK_SCALE + rel_f * neg_slope_dist, NEG_INF)
    return s, jnp.max(s, axis=-1, keepdims=True)


def _attn_a_kernel(slopes_ref,
                   q1_ref, k1_ref, v1_ref, q4_ref, k4_ref, v4_ref, q16_ref, k16_ref, v16_ref,
                   o_ref, acc_ref, m_ref, l_ref, bias_ref, bias16_ref):
    hp = pl.program_id(1)
    seq = q1_ref.shape[1]
    lane = lax.broadcasted_iota(jnp.int32, (1, LANES), 1)
    head0 = lane < HEAD_DIM
    neg_slopes = (-slopes_ref[4 * hp], -slopes_ref[4 * hp + 2])

    def bias_table(nk, lead, dil):
        qi = lax.broadcasted_iota(jnp.int32, (BLOCK, nk), 0)
        kj = lax.broadcasted_iota(jnp.int32, (BLOCK, nk), 1)
        rel = qi - kj if lead else qi + (nk - BLOCK) - kj
        valid = (rel >= 0) & (rel <= A_WINDOW)
        rel_f = rel.astype(F32)
        return jnp.concatenate(
            [jnp.where(valid, rel_f * (ns * float(dil)), NEG_INF) for ns in neg_slopes], axis=0)

    for bi, dil in enumerate(DILATIONS[:2]):
        bias_ref[bi, 0] = bias_table(2 * BLOCK, False, dil)
        bias_ref[bi, 1] = bias_table(2 * BLOCK, True, dil)
    bias16_ref[...] = bias_table(BLOCK, True, DILATIONS[2])

    def unit(q, k, v, bias):
        zero = jnp.zeros_like(q)
        qs = jnp.concatenate([jnp.where(head0, q, zero), jnp.where(head0, zero, q)], axis=0)
        s = _dot_nt(qs * QK_SCALE, k) + bias
        m = jnp.max(s, axis=-1, keepdims=True)
        e = jnp.exp(s - m)
        l = jnp.sum(e, axis=-1, keepdims=True)
        pv = _dot(e.astype(BF16), v)
        return (jnp.where(head0, pv[:BLOCK], pv[BLOCK:]),
                jnp.where(head0, m[:BLOCK], m[BLOCK:]),
                jnp.where(head0, l[:BLOCK], l[BLOCK:]))

    def store(branch, rows, res):
        acc_ref[branch, rows, :] = res[0]
        m_ref[branch, rows, :] = res[1]
        l_ref[branch, rows, :] = res[2]

    unroll = 4

    def body1(g, carry):
        for i in range(unroll):
            blk = g * unroll + i
            q0 = pl.multiple_of(blk * BLOCK, BLOCK)
            if i == 0:
                k0 = pl.multiple_of(jnp.maximum(blk - 1, 0) * BLOCK, BLOCK)
                bias = bias_ref[0, jnp.where(blk == 0, 1, 0)]
            else:
                k0 = pl.multiple_of((blk - 1) * BLOCK, BLOCK)
                bias = bias_ref[0, 0]
            store(0, pl.ds(q0, BLOCK),
                  unit(q1_ref[0, pl.ds(q0, BLOCK), :], k1_ref[0, pl.ds(k0, 2 * BLOCK), :],
                       v1_ref[0, pl.ds(k0, 2 * BLOCK), :], bias))
        return carry

    lax.fori_loop(0, seq // BLOCK // unroll, body1, 0)

    def body4(r, carry):
        for blk in range(seq // 4 // BLOCK):
            k0 = max(blk - 1, 0) * BLOCK
            store(1, pl.ds(r + 4 * blk * BLOCK, BLOCK, stride=4),
                  unit(q4_ref[0, r, pl.ds(blk * BLOCK, BLOCK), :],
                       k4_ref[0, r, pl.ds(k0, 2 * BLOCK), :],
                       v4_ref[0, r, pl.ds(k0, 2 * BLOCK), :], bias_ref[1, 1 if blk == 0 else 0]))
        return carry

    lax.fori_loop(0, 4, body4, 0)

    def body16(g, carry):
        for i in range(unroll):
            r = g * unroll + i
            store(2, pl.ds(r, BLOCK, stride=16),
                  unit(q16_ref[0, r], k16_ref[0, r], v16_ref[0, r], bias16_ref[...]))
        return carry

    lax.fori_loop(0, 16 // unroll, body16, 0)

    def merge(blk, carry):
        rows = pl.ds(pl.multiple_of(blk * BLOCK, BLOCK), BLOCK)
        m0, m1, m2 = m_ref[0, rows, :], m_ref[1, rows, :], m_ref[2, rows, :]
        mx = jnp.maximum(jnp.maximum(m0, m1), m2)
        w0, w1, w2 = jnp.exp(m0 - mx), jnp.exp(m1 - mx), jnp.exp(m2 - mx)
        num = w0 * acc_ref[0, rows, :] + w1 * acc_ref[1, rows, :] + w2 * acc_ref[2, rows, :]
        den = w0 * l_ref[0, rows, :] + w1 * l_ref[1, rows, :] + w2 * l_ref[2, rows, :]
        o_ref[0, rows, :] = num / den
        return carry

    lax.fori_loop(0, seq // BLOCK, merge, 0)


def _attn_a(slopes, nat, a4, a16):
    bsz, seq, _ = nat.shape
    pairs = A_WIDTH // LANES

    def col(base):
        return lambda b, hp: (b, 0, base + hp)

    def col4(base):
        return lambda b, hp: (b, 0, 0, base + hp)

    nat_spec = [pl.BlockSpec((1, seq, LANES), col(i * pairs)) for i in range(3)]
    a4_spec = [pl.BlockSpec((1, 4, seq // 4, LANES), col4(i * pairs)) for i in range(3)]
    a16_spec = [pl.BlockSpec((1, 16, seq // 16, LANES), col4(i * pairs)) for i in range(3)]
    return pl.pallas_call(
        _attn_a_kernel,
        grid=(bsz, pairs),
        in_specs=[pl.BlockSpec(memory_space=pltpu.SMEM)] + nat_spec + a4_spec + a16_spec,
        out_specs=pl.BlockSpec((1, seq, LANES), lambda b, hp: (b, 0, hp)),
        out_shape=jax.ShapeDtypeStruct((bsz, seq, A_WIDTH), F32),
        scratch_shapes=[pltpu.VMEM((3, seq, LANES), F32)] * 3 + [
            pltpu.VMEM((2, 2, 2 * BLOCK, 2 * BLOCK), F32), pltpu.VMEM((2 * BLOCK, BLOCK), F32)],
        compiler_params=_params("parallel", "parallel"),
        name="attn_a",
    )(slopes, nat, nat, nat, a4, a4, a4, a16, a16, a16)


def _attn_b_kernel(slopes_ref, sinks_ref, q_ref, k_ref, v_ref, o_ref, kd_ref, vd_ref):
    g = pl.program_id(1)
    seq = q_ref.shape[1]
    lane = lax.broadcasted_iota(jnp.int32, (1, LANES), 1)
    head0 = lane < HEAD_DIM
    mine = (lane >= HEAD_DIM).astype(jnp.int32) == g

    def both_halves(ref):
        x = ref[0].astype(F32)
        return jnp.where(mine, x, pltpu.roll(x, HEAD_DIM, 1)).astype(BF16)

    kd_ref[...] = both_halves(k_ref)
    vd_ref[...] = both_halves(v_ref)

    neg_slopes = [-slopes_ref[2 * (g * B_GROUP + j) + 1] for j in range(B_GROUP)]
    sinks = [sinks_ref[g * B_GROUP + j] for j in range(B_GROUP)]
    rel2, valid2 = _band_consts(2 * BLOCK, B_WINDOW)
    rel1, valid1 = _band_consts(BLOCK, B_WINDOW)

    def unit(q0, k, v, rel_f, valid):
        q = q_ref[0, pl.ds(q0, BLOCK), :]
        stacked = []
        for j in range(B_GROUP):
            qp = q[:, (j // 2) * LANES:(j // 2 + 1) * LANES]
            stacked.append(jnp.where(head0 if j % 2 == 0 else ~head0, qp, jnp.zeros_like(qp)))
        s_all = _dot_nt(jnp.concatenate(stacked, axis=0), k)
        probs, denoms = [], []
        for j in range(B_GROUP):
            s, m = _softmax_parts(s_all[j * BLOCK:(j + 1) * BLOCK], rel_f, valid, neg_slopes[j])
            m = jnp.maximum(m, sinks[j])
            e = jnp.exp(s - m)
            denoms.append(jnp.sum(e, axis=-1, keepdims=True) + jnp.exp(sinks[j] - m))
            probs.append(e.astype(BF16))
        pv = _dot(jnp.concatenate(probs, axis=0), v)
        for p in range(B_GROUP // 2):
            even = pv[(2 * p) * BLOCK:(2 * p + 1) * BLOCK] / denoms[2 * p]
            odd = pv[(2 * p + 1) * BLOCK:(2 * p + 2) * BLOCK] / denoms[2 * p + 1]
            o_ref[0, pl.ds(q0, BLOCK), p * LANES:(p + 1) * LANES] = jnp.where(head0, even, odd)

    unit(0, kd_ref[pl.ds(0, BLOCK), :], vd_ref[pl.ds(0, BLOCK), :], rel1, valid1)

    def body(blk, carry):
        q0 = pl.multiple_of(blk * BLOCK, BLOCK)
        k0 = pl.multiple_of((blk - 1) * BLOCK, BLOCK)
        unit(q0, kd_ref[pl.ds(k0, 2 * BLOCK), :], vd_ref[pl.ds(k0, 2 * BLOCK), :], rel2, valid2)
        return carry

    lax.fori_loop(1, seq // BLOCK, body, 0)


def _attn_b(slopes, sinks, zb):
    bsz, seq, _ = zb.shape
    qw = B_GROUP * HEAD_DIM
    k_blk = B_WIDTH // LANES
    return pl.pallas_call(
        _attn_b_kernel,
        grid=(bsz, B_KV_HEADS),
        in_specs=[
            pl.BlockSpec(memory_space=pltpu.SMEM),
            pl.BlockSpec(memory_space=pltpu.SMEM),
            pl.BlockSpec((1, seq, qw), lambda b, g: (b, 0, g)),
            pl.BlockSpec((1, seq, LANES), lambda b, g: (b, 0, k_blk)),
            pl.BlockSpec((1, seq, LANES), lambda b, g: (b, 0, k_blk + 1)),
        ],
        out_specs=pl.BlockSpec((1, seq, qw), lambda b, g: (b, 0, g)),
        out_shape=jax.ShapeDtypeStruct((bsz, seq, B_WIDTH), F32),
        scratch_shapes=[pltpu.VMEM((seq, LANES), BF16)] * 2,
        compiler_params=_params("parallel", "parallel"),
        name="attn_b",
    )(slopes, sinks, zb, zb, zb)


def _out_kernel(h_ref, oa_ref, ob_ref, ga_ref, gb_ref, wo_ref, bo_ref, gpost_ref, o_ref):
    na = _rms(oa_ref[...], ga_ref[...]).astype(BF16)
    nb = _rms(ob_ref[...], gb_ref[...]).astype(BF16)
    o = _dot(na, wo_ref[0:A_WIDTH, :]) + _dot(nb, wo_ref[A_WIDTH:A_WIDTH + B_WIDTH, :]) + bo_ref[...]
    o_ref[...] = h_ref[...] + _rms(o, gpost_ref[...])


def _out_proj(h, oa, ob, g_a, g_b, w_o, b_o, g_post):
    t, d = h.shape
    tm = OUT_TM
    row = lambda i: (i, 0)
    return pl.pallas_call(
        _out_kernel,
        grid=(t // tm,),
        in_specs=[
            pl.BlockSpec((tm, d), row),
            pl.BlockSpec((tm, A_WIDTH), row),
            pl.BlockSpec((tm, B_WIDTH), row),
            _resident((1, A_WIDTH)),
            _resident((1, B_WIDTH)),
            _resident((A_WIDTH + B_WIDTH, d)),
            _resident((1, d)),
            _resident((1, d)),
        ],
        out_specs=pl.BlockSpec((tm, d), row),
        out_shape=jax.ShapeDtypeStruct((t, d), F32),
        compiler_params=_params("parallel"),
        name="out_proj",
    )(h, oa, ob, g_a, g_b, w_o, b_o, g_post)


def _ple_kernel(h_ref, p_ref, gpre_ref, wg_ref, wp_ref, gpost_ref, o_ref):
    h = h_ref[...]
    gate = jax.nn.sigmoid(_dot(_rms(h, gpre_ref[...]).astype(BF16), wg_ref[...]))
    e = _dot(p_ref[...].astype(BF16), wp_ref[...])
    o_ref[...] = h + _rms(gate * e, gpost_ref[...])


def _ple(h, p, g_pre, w_gate, w_proj, g_post):
    t, d = h.shape
    pd = p.shape[1]
    tm = PLE_TM
    row = lambda i: (i, 0)
    return pl.pallas_call(
        _ple_kernel,
        grid=(t // tm,),
        in_specs=[
            pl.BlockSpec((tm, d), row),
            pl.BlockSpec((tm, pd), row),
            _resident((1, d)),
            _resident((d, d)),
            _resident((pd, d)),
            _resident((1, d)),
        ],
        out_specs=pl.BlockSpec((tm, d), row),
        out_shape=jax.ShapeDtypeStruct((t, d), F32),
        compiler_params=_params("parallel"),
        name="ple",
    )(h, p, g_pre, w_gate, w_proj, g_post)


def _alibi_slopes(n):
    return jnp.exp2(-ALIBI_MAX_BIAS * (jnp.arange(n, dtype=F32) + 1.0) / n)


def kernel(x, p, g_ffn1_pre, w_ffn1_gate, w_ffn1_up, w_ffn1_down, g_ffn1_post, g_mix_pre, w_qkv, b_qkv, attn_sinks, g_out_a, g_out_b, w_o, b_o, g_mix_post, g_ffn2_pre, w_ffn2_gate, w_ffn2_up, w_ffn2_down, g_ffn2_post, g_ple_pre, w_ple_gate, w_ple_proj, g_ple_post):
    bsz, seq, d = x.shape
    depth = p.shape[0]
    t = bsz * seq
    slopes = _alibi_slopes(A_HEADS + B_Q_HEADS)
    bf = lambda w: w.astype(BF16)
    h = x.reshape(t, d)
    for i in range(depth):
        h = _ffn(h, g_ffn1_pre[i][None], bf(w_ffn1_gate[i]), bf(w_ffn1_up[i]), bf(w_ffn1_down[i]),
                 g_ffn1_post[i][None])
        wq = bf(w_qkv[i])
        nat, a4, a16, zb = _qkv(h, g_mix_pre[i][None], wq[:, :3 * A_WIDTH], b_qkv[i][None, :3 * A_WIDTH],
                                wq[:, 3 * A_WIDTH:], b_qkv[i][None, 3 * A_WIDTH:], bsz, seq)
        oa = _attn_a(slopes, nat.reshape(bsz, seq, 3 * A_WIDTH), a4, a16)
        ob = _attn_b(slopes, attn_sinks[i].reshape(-1), zb.reshape(bsz, seq, ZB_WIDTH))
        h = _out_proj(h, oa.reshape(t, A_WIDTH), ob.reshape(t, B_WIDTH), g_out_a[i][None], g_out_b[i][None],
                      bf(w_o[i]), b_o[i][None], g_mix_post[i][None])
        h = _ffn(h, g_ffn2_pre[i][None], bf(w_ffn2_gate[i]), bf(w_ffn2_up[i]), bf(w_ffn2_down[i]),
                 g_ffn2_post[i][None])
        h = _ple(h, p[i].reshape(t, -1), g_ple_pre[i][None], bf(w_ple_gate[i]), bf(w_ple_proj[i]),
                 g_ple_post[i][None])
    return h.reshape(bsz, seq, d)
```

```python
import functools

import jax
import jax.numpy as jnp
from jax import lax
from jax.experimental import pallas as pl
from jax.experimental.pallas import tpu as pltpu

F32 = jnp.float32
BF16 = jnp.bfloat16

HEAD_DIM = 64
LANES = 128
MXU_WIDTH = 256
BLOCK = 128
A_HEADS = 16
B_Q_HEADS = 16
B_KV_HEADS = 2
B_GROUP = B_Q_HEADS // B_KV_HEADS
A_WIDTH = A_HEADS * HEAD_DIM
B_WIDTH = B_Q_HEADS * HEAD_DIM
KV_B_WIDTH = B_KV_HEADS * HEAD_DIM
ZB_WIDTH = B_WIDTH + 2 * KV_B_WIDTH
DILATIONS = (1, 4, 16)
A_WINDOW = 128
B_WINDOW = 127
ALIBI_MAX_BIAS = 8.0
EPS = 1e-6
QK_SCALE = HEAD_DIM ** -0.5
NEG_INF = float("-inf")

VMEM_LIMIT = 61 * 1024 * 1024

FFN_TM = 1024
FFN_TF = 256
QKV_TM = 512
ATTN_A_UNITS_PER_ITER = 16
ATTN_B_UNITS_PER_ITER = 4
OUT_TM = 512
PLE_TM = 512


def _rms(x, g):
    return x * lax.rsqrt(jnp.mean(x * x, axis=-1, keepdims=True) + EPS) * g


def _dot(a, b):
    return jnp.dot(a, b, preferred_element_type=F32)


def _dot_nt(a, b):
    return lax.dot_general(a, b, (((1,), (1,)), ((), ())), preferred_element_type=F32)


def _resident(shape):
    return pl.BlockSpec(shape, lambda *_: (0,) * len(shape), pipeline_mode=pl.Buffered(1))


def _params(*semantics):
    return pltpu.CompilerParams(dimension_semantics=semantics, vmem_limit_bytes=VMEM_LIMIT)


def _ffn_kernel(x_ref, gpre_ref, wg_ref, wu_ref, wd_ref, gpost_ref, o_ref, xn_ref):
    j = pl.program_id(1)

    @pl.when(j == 0)
    def _():
        xn_ref[...] = _rms(x_ref[...], gpre_ref[...]).astype(BF16)
        o_ref[...] = jnp.zeros_like(o_ref)

    xn = xn_ref[...]
    gate = _dot(xn, wg_ref[...])
    up = _dot(xn, wu_ref[...])
    act = (gate * jax.nn.sigmoid(gate) * up).astype(BF16)
    o_ref[...] += _dot(act, wd_ref[...])

    @pl.when(j == pl.num_programs(1) - 1)
    def _():
        o_ref[...] = x_ref[...] + 0.5 * _rms(o_ref[...], gpost_ref[...])


def _ffn(h, g_pre, w_gate, w_up, w_down, g_post):
    t, d = h.shape
    d_ff = w_gate.shape[1]
    tm, tf = FFN_TM, FFN_TF
    return pl.pallas_call(
        _ffn_kernel,
        grid=(t // tm, d_ff // tf),
        in_specs=[
            pl.BlockSpec((tm, d), lambda i, j: (i, 0)),
            _resident((1, d)),
            pl.BlockSpec((d, tf), lambda i, j: (0, j)),
            pl.BlockSpec((d, tf), lambda i, j: (0, j)),
            pl.BlockSpec((tf, d), lambda i, j: (j, 0)),
            _resident((1, d)),
        ],
        out_specs=pl.BlockSpec((tm, d), lambda i, j: (i, 0)),
        out_shape=jax.ShapeDtypeStruct((t, d), F32),
        scratch_shapes=[pltpu.VMEM((tm, d), BF16)],
        compiler_params=_params("parallel", "arbitrary"),
        name="ffn",
    )(h, g_pre, w_gate, w_up, w_down, g_post)


def _qkv_kernel(h_ref, g_ref, wa_ref, ba_ref, wb_ref, bb_ref,
                nat_ref, a4_ref, a16_ref, zb_ref, zs_ref):
    tm = h_ref.shape[0]
    xn = _rms(h_ref[...], g_ref[...]).astype(BF16)
    zb_ref[...] = (_dot(xn, wb_ref[...]) + bb_ref[...]).astype(BF16)

    for c2 in range(wa_ref.shape[1] // MXU_WIDTH):
        wide = slice(c2 * MXU_WIDTH, (c2 + 1) * MXU_WIDTH)
        z2 = _dot(xn, wa_ref[:, wide]) + ba_ref[:, wide]
        nat_ref[:, wide] = z2.astype(BF16)
        for half in range(MXU_WIDTH // LANES):
            c = c2 * (MXU_WIDTH // LANES) + half
            cols = slice(c * LANES, (c + 1) * LANES)
            zs_ref[c] = z2[:, half * LANES:(half + 1) * LANES]
            for r in range(4):
                a4_ref[0, r, :, cols] = zs_ref[c, pl.ds(r, tm // 4, stride=4), :].astype(BF16)
            for r in range(16):
                a16_ref[0, r, :, cols] = zs_ref[c, pl.ds(r, tm // 16, stride=16), :].astype(BF16)


def _qkv(h, g, w_a, b_a, w_b, b_b, bsz, seq):
    t, d = h.shape
    tm = QKV_TM
    tn = 3 * A_WIDTH
    per_seq = seq // tm
    return pl.pallas_call(
        _qkv_kernel,
        grid=(t // tm,),
        in_specs=[
            pl.BlockSpec((tm, d), lambda i: (i, 0)),
            _resident((1, d)),
            _resident((d, tn)),
            _resident((1, tn)),
            _resident((d, ZB_WIDTH)),
            _resident((1, ZB_WIDTH)),
        ],
        out_specs=[
            pl.BlockSpec((tm, tn), lambda i: (i, 0)),
            pl.BlockSpec((1, 4, tm // 4, tn), lambda i: (i // per_seq, 0, i % per_seq, 0)),
            pl.BlockSpec((1, 16, tm // 16, tn), lambda i: (i // per_seq, 0, i % per_seq, 0)),
            pl.BlockSpec((tm, ZB_WIDTH), lambda i: (i, 0)),
        ],
        out_shape=[
            jax.ShapeDtypeStruct((t, 3 * A_WIDTH), BF16),
            jax.ShapeDtypeStruct((bsz, 4, seq // 4, 3 * A_WIDTH), BF16),
            jax.ShapeDtypeStruct((bsz, 16, seq // 16, 3 * A_WIDTH), BF16),
            jax.ShapeDtypeStruct((t, ZB_WIDTH), BF16),
        ],
        scratch_shapes=[pltpu.VMEM((tn // LANES, tm, LANES), F32)],
        compiler_params=_params("parallel"),
        name="qkv",
    )(h, g, w_a, b_a, w_b, b_b)


def _attn_a_kernel(slopes_ref,
                   q1_ref, k1_ref, v1_ref, q4_ref, k4_ref, v4_ref, q16_ref, k16_ref, v16_ref,
                   o_ref, acc_ref, m_ref, l_ref, bias_ref, bias16_ref):
    hp = pl.program_id(1)
    seq = q1_ref.shape[1]
    lane = lax.broadcasted_iota(jnp.int32, (1, LANES), 1)
    head0 = lane < HEAD_DIM
    neg_slopes = (-slopes_ref[4 * hp], -slopes_ref[4 * hp + 2])

    def bias_table(nk, lead, dil):
        qi = lax.broadcasted_iota(jnp.int32, (BLOCK, nk), 0)
        kj = lax.broadcasted_iota(jnp.int32, (BLOCK, nk), 1)
        rel = qi - kj if lead else qi + (nk - BLOCK) - kj
        valid = (rel >= 0) & (rel <= A_WINDOW)
        rel_f = rel.astype(F32)
        return jnp.concatenate(
            [jnp.where(valid, rel_f * (ns * float(dil)), NEG_INF) for ns in neg_slopes], axis=0)

    for bi, dil in enumerate(DILATIONS[:2]):
        bias_ref[bi, 0] = bias_table(2 * BLOCK, False, dil)
        bias_ref[bi, 1] = bias_table(2 * BLOCK, True, dil)
    bias16_ref[...] = bias_table(BLOCK, True, DILATIONS[2])

    def unit(q, k, v, bias):
        zero = jnp.zeros_like(q)
        qs = jnp.concatenate([jnp.where(head0, q, zero), jnp.where(head0, zero, q)], axis=0)
        s = _dot_nt(qs * QK_SCALE, k) + bias
        m = jnp.max(s, axis=-1, keepdims=True)
        e = jnp.exp(s - m)
        l = jnp.sum(e, axis=-1, keepdims=True)
        pv = _dot(e.astype(BF16), v)
        return (jnp.where(head0, pv[:BLOCK], pv[BLOCK:]),
                jnp.where(head0, m[:BLOCK], m[BLOCK:]),
                jnp.where(head0, l[:BLOCK], l[BLOCK:]))

    def store(branch, rows, res):
        acc_ref[branch, rows, :] = res[0]
        m_ref[branch, rows, :] = res[1]
        l_ref[branch, rows, :] = res[2]

    unroll = ATTN_A_UNITS_PER_ITER
    nb4 = seq // 4 // BLOCK

    def body1(g, carry):
        for i in range(unroll):
            blk = g * unroll + i
            q0 = pl.multiple_of(blk * BLOCK, BLOCK)
            if i == 0:
                k0 = pl.multiple_of(jnp.maximum(blk - 1, 0) * BLOCK, BLOCK)
                bias = bias_ref[0, jnp.where(blk == 0, 1, 0)]
            else:
                k0 = pl.multiple_of((blk - 1) * BLOCK, BLOCK)
                bias = bias_ref[0, 0]
            store(0, pl.ds(q0, BLOCK),
                  unit(q1_ref[0, pl.ds(q0, BLOCK), :], k1_ref[0, pl.ds(k0, 2 * BLOCK), :],
                       v1_ref[0, pl.ds(k0, 2 * BLOCK), :], bias))
        return carry

    lax.fori_loop(0, seq // BLOCK // unroll, body1, 0)

    def body4(g, carry):
        for i in range(unroll):
            r = g * (unroll // nb4) + i // nb4
            blk = i % nb4
            k0 = max(blk - 1, 0) * BLOCK
            store(1, pl.ds(r + 4 * blk * BLOCK, BLOCK, stride=4),
                  unit(q4_ref[0, r, pl.ds(blk * BLOCK, BLOCK), :],
                       k4_ref[0, r, pl.ds(k0, 2 * BLOCK), :],
                       v4_ref[0, r, pl.ds(k0, 2 * BLOCK), :], bias_ref[1, 1 if blk == 0 else 0]))
        return carry

    lax.fori_loop(0, 4 * nb4 // unroll, body4, 0)

    def body16(g, carry):
        for i in range(unroll):
            r = g * unroll + i
            store(2, pl.ds(r, BLOCK, stride=16),
                  unit(q16_ref[0, r], k16_ref[0, r], v16_ref[0, r], bias16_ref[...]))
        return carry

    lax.fori_loop(0, 16 // unroll, body16, 0)

    def merge(blk, carry):
        rows = pl.ds(pl.multiple_of(blk * (2 * BLOCK), 2 * BLOCK), 2 * BLOCK)
        m0, m1, m2 = m_ref[0, rows, :], m_ref[1, rows, :], m_ref[2, rows, :]
        mx = jnp.maximum(jnp.maximum(m0, m1), m2)
        w0, w1, w2 = jnp.exp(m0 - mx), jnp.exp(m1 - mx), jnp.exp(m2 - mx)
        num = w0 * acc_ref[0, rows, :] + w1 * acc_ref[1, rows, :] + w2 * acc_ref[2, rows, :]
        den = w0 * l_ref[0, rows, :] + w1 * l_ref[1, rows, :] + w2 * l_ref[2, rows, :]
        o_ref[0, rows, :] = num / den
        return carry

    lax.fori_loop(0, seq // (2 * BLOCK), merge, 0)


def _attn_a(slopes, nat, a4, a16):
    bsz, seq, _ = nat.shape
    pairs = A_WIDTH // LANES

    def col(base):
        return lambda b, hp: (b, 0, base + hp)

    def col4(base):
        return lambda b, hp: (b, 0, 0, base + hp)

    nat_spec = [pl.BlockSpec((1, seq, LANES), col(i * pairs)) for i in range(3)]
    a4_spec = [pl.BlockSpec((1, 4, seq // 4, LANES), col4(i * pairs)) for i in range(3)]
    a16_spec = [pl.BlockSpec((1, 16, seq // 16, LANES), col4(i * pairs)) for i in range(3)]
    return pl.pallas_call(
        _attn_a_kernel,
        grid=(bsz, pairs),
        in_specs=[pl.BlockSpec(memory_space=pltpu.SMEM)] + nat_spec + a4_spec + a16_spec,
        out_specs=pl.BlockSpec((1, seq, LANES), lambda b, hp: (b, 0, hp)),
        out_shape=jax.ShapeDtypeStruct((bsz, seq, A_WIDTH), F32),
        scratch_shapes=[pltpu.VMEM((3, seq, LANES), F32)] * 3 + [
            pltpu.VMEM((2, 2, 2 * BLOCK, 2 * BLOCK), F32), pltpu.VMEM((2 * BLOCK, BLOCK), F32)],
        compiler_params=_params("parallel", "parallel"),
        name="attn_a",
    )(slopes, nat, nat, nat, a4, a4, a4, a16, a16, a16)


def _attn_b_kernel(slopes_ref, sinks_ref, q_ref, k_ref, v_ref, o_ref, kd_ref, vd_ref, bias_ref):
    g = pl.program_id(1)
    seq = q_ref.shape[1]
    lane = lax.broadcasted_iota(jnp.int32, (1, LANES), 1)
    head0 = lane < HEAD_DIM
    mine = (lane >= HEAD_DIM).astype(jnp.int32) == g

    def both_halves(ref):
        x = ref[0].astype(F32)
        return jnp.where(mine, x, pltpu.roll(x, HEAD_DIM, 1)).astype(BF16)

    kd_ref[...] = both_halves(k_ref)
    vd_ref[...] = both_halves(v_ref)

    neg_slopes = [-slopes_ref[2 * (g * B_GROUP + j) + 1] for j in range(B_GROUP)]
    sinks = [sinks_ref[g * B_GROUP + j] for j in range(B_GROUP)]

    for lead in (False, True):
        qi = lax.broadcasted_iota(jnp.int32, (BLOCK, 2 * BLOCK), 0)
        kj = lax.broadcasted_iota(jnp.int32, (BLOCK, 2 * BLOCK), 1)
        rel = qi - kj if lead else qi + BLOCK - kj
        valid = (rel >= 0) & (rel <= B_WINDOW)
        rel_f = rel.astype(F32)
        for j in range(B_GROUP):
            bias_ref[int(lead), pl.ds(j * BLOCK, BLOCK), :] = jnp.where(valid, rel_f * neg_slopes[j], NEG_INF)

    def unit(blk, lead_possible):
        q0 = pl.multiple_of(blk * BLOCK, BLOCK)
        if lead_possible:
            k0 = pl.multiple_of(jnp.maximum(blk - 1, 0) * BLOCK, BLOCK)
            bias = bias_ref[jnp.where(blk == 0, 1, 0)]
        else:
            k0 = pl.multiple_of((blk - 1) * BLOCK, BLOCK)
            bias = bias_ref[0]
        k = kd_ref[pl.ds(k0, 2 * BLOCK), :]
        v = vd_ref[pl.ds(k0, 2 * BLOCK), :]
        q = q_ref[0, pl.ds(q0, BLOCK), :] * QK_SCALE
        stacked = []
        for j in range(B_GROUP):
            qp = q[:, (j // 2) * LANES:(j // 2 + 1) * LANES]
            stacked.append(jnp.where(head0 if j % 2 == 0 else ~head0, qp, jnp.zeros_like(qp)))
        s_all = _dot_nt(jnp.concatenate(stacked, axis=0), k) + bias
        probs, inv = [], []
        for j in range(B_GROUP):
            s = s_all[j * BLOCK:(j + 1) * BLOCK]
            m = jnp.maximum(jnp.max(s, axis=-1, keepdims=True), sinks[j])
            e = jnp.exp(s - m)
            inv.append(1.0 / (jnp.sum(e, axis=-1, keepdims=True) + jnp.exp(sinks[j] - m)))
            probs.append(e.astype(BF16))
        pv = _dot(jnp.concatenate(probs, axis=0), v)
        for p in range(B_GROUP // 2):
            even = pv[(2 * p) * BLOCK:(2 * p + 1) * BLOCK]
            odd = pv[(2 * p + 1) * BLOCK:(2 * p + 2) * BLOCK]
            o_ref[0, pl.ds(q0, BLOCK), p * LANES:(p + 1) * LANES] = (
                jnp.where(head0, even, odd) * jnp.where(head0, inv[2 * p], inv[2 * p + 1]))

    def body(it, carry):
        for i in range(ATTN_B_UNITS_PER_ITER):
            unit(it * ATTN_B_UNITS_PER_ITER + i, i == 0)
        return carry

    lax.fori_loop(0, seq // BLOCK // ATTN_B_UNITS_PER_ITER, body, 0)


def _attn_b(slopes, sinks, zb):
    bsz, seq, _ = zb.shape
    qw = B_GROUP * HEAD_DIM
    k_blk = B_WIDTH // LANES
    return pl.pallas_call(
        _attn_b_kernel,
        grid=(bsz, B_KV_HEADS),
        in_specs=[
            pl.BlockSpec(memory_space=pltpu.SMEM),
            pl.BlockSpec(memory_space=pltpu.SMEM),
            pl.BlockSpec((1, seq, qw), lambda b, g: (b, 0, g)),
            pl.BlockSpec((1, seq, LANES), lambda b, g: (b, 0, k_blk)),
            pl.BlockSpec((1, seq, LANES), lambda b, g: (b, 0, k_blk + 1)),
        ],
        out_specs=pl.BlockSpec((1, seq, qw), lambda b, g: (b, 0, g)),
        out_shape=jax.ShapeDtypeStruct((bsz, seq, B_WIDTH), F32),
        scratch_shapes=[pltpu.VMEM((seq, LANES), BF16)] * 2 + [
            pltpu.VMEM((2, B_GROUP * BLOCK, 2 * BLOCK), F32)],
        compiler_params=_params("parallel", "parallel"),
        name="attn_b",
    )(slopes, sinks, zb, zb, zb)


def _out_kernel(h_ref, oa_ref, ob_ref, ga_ref, gb_ref, wo_ref, bo_ref, gpost_ref, o_ref):
    na = _rms(oa_ref[...], ga_ref[...]).astype(BF16)
    nb = _rms(ob_ref[...], gb_ref[...]).astype(BF16)
    o = _dot(na, wo_ref[0:A_WIDTH, :]) + _dot(nb, wo_ref[A_WIDTH:A_WIDTH + B_WIDTH, :]) + bo_ref[...]
    o_ref[...] = h_ref[...] + _rms(o, gpost_ref[...])


def _out_proj(h, oa, ob, g_a, g_b, w_o, b_o, g_post):
    t, d = h.shape
    tm = OUT_TM
    row = lambda i: (i, 0)
    return pl.pallas_call(
        _out_kernel,
        grid=(t // tm,),
        in_specs=[
            pl.BlockSpec((tm, d), row),
            pl.BlockSpec((tm, A_WIDTH), row),
            pl.BlockSpec((tm, B_WIDTH), row),
            _resident((1, A_WIDTH)),
            _resident((1, B_WIDTH)),
            _resident((A_WIDTH + B_WIDTH, d)),
            _resident((1, d)),
            _resident((1, d)),
        ],
        out_specs=pl.BlockSpec((tm, d), row),
        out_shape=jax.ShapeDtypeStruct((t, d), F32),
        compiler_params=_params("parallel"),
        name="out_proj",
    )(h, oa, ob, g_a, g_b, w_o, b_o, g_post)


def _ple_kernel(h_ref, p_ref, gpre_ref, wg_ref, wp_ref, gpost_ref, o_ref):
    h = h_ref[...]
    gate = jax.nn.sigmoid(_dot(_rms(h, gpre_ref[...]).astype(BF16), wg_ref[...]))
    e = _dot(p_ref[...].astype(BF16), wp_ref[...])
    o_ref[...] = h + _rms(gate * e, gpost_ref[...])


def _ple(h, p, g_pre, w_gate, w_proj, g_post):
    t, d = h.shape
    pd = p.shape[1]
    tm = PLE_TM
    row = lambda i: (i, 0)
    return pl.pallas_call(
        _ple_kernel,
        grid=(t // tm,),
        in_specs=[
            pl.BlockSpec((tm, d), row),
            pl.BlockSpec((tm, pd), row),
            _resident((1, d)),
            _resident((d, d)),
            _resident((pd, d)),
            _resident((1, d)),
        ],
        out_specs=pl.BlockSpec((tm, d), row),
        out_shape=jax.ShapeDtypeStruct((t, d), F32),
        compiler_params=_params("parallel"),
        name="ple",
    )(h, p, g_pre, w_gate, w_proj, g_post)


def _alibi_slopes(n):
    return jnp.exp2(-ALIBI_MAX_BIAS * (jnp.arange(n, dtype=F32) + 1.0) / n)


def kernel(x, p, g_ffn1_pre, w_ffn1_gate, w_ffn1_up, w_ffn1_down, g_ffn1_post, g_mix_pre, w_qkv, b_qkv, attn_sinks, g_out_a, g_out_b, w_o, b_o, g_mix_post, g_ffn2_pre, w_ffn2_gate, w_ffn2_up, w_ffn2_down, g_ffn2_post, g_ple_pre, w_ple_gate, w_ple_proj, g_ple_post):
    bsz, seq, d = x.shape
    depth = p.shape[0]
    t = bsz * seq
    slopes = _alibi_slopes(A_HEADS + B_Q_HEADS)
    bf = lambda w: w.astype(BF16)
    h = x.reshape(t, d)
    for i in range(depth):
        h = _ffn(h, g_ffn1_pre[i][None], bf(w_ffn1_gate[i]), bf(w_ffn1_up[i]), bf(w_ffn1_down[i]),
                 g_ffn1_post[i][None])
        wq = bf(w_qkv[i])
        nat, a4, a16, zb = _qkv(h, g_mix_pre[i][None], wq[:, :3 * A_WIDTH], b_qkv[i][None, :3 * A_WIDTH],
                                wq[:, 3 * A_WIDTH:], b_qkv[i][None, 3 * A_WIDTH:], bsz, seq)
        oa = _attn_a(slopes, nat.reshape(bsz, seq, 3 * A_WIDTH), a4, a16)
        ob = _attn_b(slopes, attn_sinks[i].reshape(-1), zb.reshape(bsz, seq, ZB_WIDTH))
        h = _out_proj(h, oa.reshape(t, A_WIDTH), ob.reshape(t, B_WIDTH), g_out_a[i][None], g_out_b[i][None],
                      bf(w_o[i]), b_o[i][None], g_mix_post[i][None])
        h = _ffn(h, g_ffn2_pre[i][None], bf(w_ffn2_gate[i]), bf(w_ffn2_up[i]), bf(w_ffn2_down[i]),
                 g_ffn2_post[i][None])
        h = _ple(h, p[i].reshape(t, -1), g_ple_pre[i][None], bf(w_ple_gate[i]), bf(w_ple_proj[i]),
                 g_ple_post[i][None])
    return h.reshape(bsz, seq, d)
```

```python
import functools

import jax
import jax.numpy as jnp
from jax import lax
from jax.experimental import pallas as pl
from jax.experimental.pallas import tpu as pltpu

F32 = jnp.float32
BF16 = jnp.bfloat16

HEAD_DIM = 64
LANES = 128
MXU_WIDTH = 256
BLOCK = 128
A_HEADS = 16
B_Q_HEADS = 16
B_KV_HEADS = 2
B_GROUP = B_Q_HEADS // B_KV_HEADS
A_WIDTH = A_HEADS * HEAD_DIM
B_WIDTH = B_Q_HEADS * HEAD_DIM
KV_B_WIDTH = B_KV_HEADS * HEAD_DIM
ZB_WIDTH = B_WIDTH + 2 * KV_B_WIDTH
DILATIONS = (1, 4, 16)
A_WINDOW = 128
B_WINDOW = 127
ALIBI_MAX_BIAS = 8.0
EPS = 1e-6
QK_SCALE = HEAD_DIM ** -0.5
NEG_INF = float("-inf")

VMEM_LIMIT = 61 * 1024 * 1024

NORM_ROWS = 256
FFN_TM = 1024
FFN_TF = 512
QKV_TM = 512
ATTN_A_UNITS_PER_ITER = 16
ATTN_B_UNITS_PER_ITER = 4
OUT_TM = 512
PLE_TM = 512


def _rms(x, g):
    return x * lax.rsqrt(jnp.mean(x * x, axis=-1, keepdims=True) + EPS) * g


def _for_row_chunks(n_rows, fn):
    def body(i, carry):
        fn(pl.ds(pl.multiple_of(i * NORM_ROWS, NORM_ROWS), NORM_ROWS))
        return carry

    lax.fori_loop(0, n_rows // NORM_ROWS, body, 0)


def _dot(a, b):
    return jnp.dot(a, b, preferred_element_type=F32)


def _dot_nt(a, b):
    return lax.dot_general(a, b, (((1,), (1,)), ((), ())), preferred_element_type=F32)


def _resident(shape):
    return pl.BlockSpec(shape, lambda *_: (0,) * len(shape), pipeline_mode=pl.Buffered(1))


def _params(*semantics):
    return pltpu.CompilerParams(dimension_semantics=semantics, vmem_limit_bytes=VMEM_LIMIT)


def _ffn_kernel(x_ref, gpre_ref, wg_ref, wu_ref, wd_ref, gpost_ref, o_ref, xn_ref):
    j = pl.program_id(1)
    tm = x_ref.shape[0]

    @pl.when(j == 0)
    def _():
        def prologue(rows):
            xn_ref[rows, :] = _rms(x_ref[rows, :], gpre_ref[...]).astype(BF16)
            o_ref[rows, :] = jnp.zeros((NORM_ROWS, o_ref.shape[1]), F32)
        _for_row_chunks(tm, prologue)

    xn = xn_ref[...]
    gate = _dot(xn, wg_ref[...])
    up = _dot(xn, wu_ref[...])
    act = (gate * jax.nn.sigmoid(gate) * up).astype(BF16)
    o_ref[...] += _dot(act, wd_ref[...])

    @pl.when(j == pl.num_programs(1) - 1)
    def _():
        def epilogue(rows):
            o_ref[rows, :] = x_ref[rows, :] + 0.5 * _rms(o_ref[rows, :], gpost_ref[...])
        _for_row_chunks(tm, epilogue)


def _ffn(h, g_pre, w_gate, w_up, w_down, g_post):
    t, d = h.shape
    d_ff = w_gate.shape[1]
    tm, tf = FFN_TM, FFN_TF
    return pl.pallas_call(
        _ffn_kernel,
        grid=(t // tm, d_ff // tf),
        in_specs=[
            pl.BlockSpec((tm, d), lambda i, j: (i, 0)),
            _resident((1, d)),
            pl.BlockSpec((d, tf), lambda i, j: (0, j)),
            pl.BlockSpec((d, tf), lambda i, j: (0, j)),
            pl.BlockSpec((tf, d), lambda i, j: (j, 0)),
            _resident((1, d)),
        ],
        out_specs=pl.BlockSpec((tm, d), lambda i, j: (i, 0)),
        out_shape=jax.ShapeDtypeStruct((t, d), F32),
        scratch_shapes=[pltpu.VMEM((tm, d), BF16)],
        compiler_params=_params("parallel", "arbitrary"),
        name="ffn",
    )(h, g_pre, w_gate, w_up, w_down, g_post)


def _qkv_kernel(h_ref, g_ref, wa_ref, ba_ref, wb_ref, bb_ref,
                nat_ref, a4_ref, a16_ref, zb_ref, zs_ref):
    tm = h_ref.shape[0]
    xn = _rms(h_ref[...], g_ref[...]).astype(BF16)
    zb_ref[...] = (_dot(xn, wb_ref[...]) + bb_ref[...]).astype(BF16)

    for c2 in range(wa_ref.shape[1] // MXU_WIDTH):
        wide = slice(c2 * MXU_WIDTH, (c2 + 1) * MXU_WIDTH)
        z2 = _dot(xn, wa_ref[:, wide]) + ba_ref[:, wide]
        nat_ref[:, wide] = z2.astype(BF16)
        for half in range(MXU_WIDTH // LANES):
            c = c2 * (MXU_WIDTH // LANES) + half
            cols = slice(c * LANES, (c + 1) * LANES)
            zs_ref[c] = z2[:, half * LANES:(half + 1) * LANES]
            for r in range(4):
                a4_ref[0, r, :, cols] = zs_ref[c, pl.ds(r, tm // 4, stride=4), :].astype(BF16)
            for r in range(16):
                a16_ref[0, r, :, cols] = zs_ref[c, pl.ds(r, tm // 16, stride=16), :].astype(BF16)


def _qkv(h, g, w_a, b_a, w_b, b_b, bsz, seq):
    t, d = h.shape
    tm = QKV_TM
    tn = 3 * A_WIDTH
    per_seq = seq // tm
    return pl.pallas_call(
        _qkv_kernel,
        grid=(t // tm,),
        in_specs=[
            pl.BlockSpec((tm, d), lambda i: (i, 0)),
            _resident((1, d)),
            _resident((d, tn)),
            _resident((1, tn)),
            _resident((d, ZB_WIDTH)),
            _resident((1, ZB_WIDTH)),
        ],
        out_specs=[
            pl.BlockSpec((tm, tn), lambda i: (i, 0)),
            pl.BlockSpec((1, 4, tm // 4, tn), lambda i: (i // per_seq, 0, i % per_seq, 0)),
            pl.BlockSpec((1, 16, tm // 16, tn), lambda i: (i // per_seq, 0, i % per_seq, 0)),
            pl.BlockSpec((tm, ZB_WIDTH), lambda i: (i, 0)),
        ],
        out_shape=[
            jax.ShapeDtypeStruct((t, 3 * A_WIDTH), BF16),
            jax.ShapeDtypeStruct((bsz, 4, seq // 4, 3 * A_WIDTH), BF16),
            jax.ShapeDtypeStruct((bsz, 16, seq // 16, 3 * A_WIDTH), BF16),
            jax.ShapeDtypeStruct((t, ZB_WIDTH), BF16),
        ],
        scratch_shapes=[pltpu.VMEM((tn // LANES, tm, LANES), F32)],
        compiler_params=_params("parallel"),
        name="qkv",
    )(h, g, w_a, b_a, w_b, b_b)


def _attn_a_kernel(slopes_ref,
                   q1_ref, k1_ref, v1_ref, q4_ref, k4_ref, v4_ref, q16_ref, k16_ref, v16_ref,
                   o_ref, acc_ref, m_ref, l_ref, bias_ref, bias16_ref):
    hp = pl.program_id(1)
    seq = q1_ref.shape[1]
    lane = lax.broadcasted_iota(jnp.int32, (1, LANES), 1)
    head0 = lane < HEAD_DIM
    neg_slopes = (-slopes_ref[4 * hp], -slopes_ref[4 * hp + 2])

    def bias_table(nk, lead, dil):
        qi = lax.broadcasted_iota(jnp.int32, (BLOCK, nk), 0)
        kj = lax.broadcasted_iota(jnp.int32, (BLOCK, nk), 1)
        rel = qi - kj if lead else qi + (nk - BLOCK) - kj
        valid = (rel >= 0) & (rel <= A_WINDOW)
        rel_f = rel.astype(F32)
        return jnp.concatenate(
            [jnp.where(valid, rel_f * (ns * float(dil)), NEG_INF) for ns in neg_slopes], axis=0)

    for bi, dil in enumerate(DILATIONS[:2]):
        bias_ref[bi, 0] = bias_table(2 * BLOCK, False, dil)
        bias_ref[bi, 1] = bias_table(2 * BLOCK, True, dil)
    bias16_ref[...] = bias_table(BLOCK, True, DILATIONS[2])

    def unit(q, k, v, bias):
        zero = jnp.zeros_like(q)
        qs = jnp.concatenate([jnp.where(head0, q, zero), jnp.where(head0, zero, q)], axis=0)
        s = _dot_nt(qs * QK_SCALE, k) + bias
        m = jnp.max(s, axis=-1, keepdims=True)
        e = jnp.exp(s - m)
        l = jnp.sum(e, axis=-1, keepdims=True)
        pv = _dot(e.astype(BF16), v)
        return (jnp.where(head0, pv[:BLOCK], pv[BLOCK:]),
                jnp.where(head0, m[:BLOCK], m[BLOCK:]),
                jnp.where(head0, l[:BLOCK], l[BLOCK:]))

    def store(branch, rows, res):
        acc_ref[branch, rows, :] = res[0]
        m_ref[branch, rows, :] = res[1]
        l_ref[branch, rows, :] = res[2]

    unroll = ATTN_A_UNITS_PER_ITER
    nb4 = seq // 4 // BLOCK

    def body1(g, carry):
        for i in range(unroll):
            blk = g * unroll + i
            q0 = pl.multiple_of(blk * BLOCK, BLOCK)
            if i == 0:
                k0 = pl.multiple_of(jnp.maximum(blk - 1, 0) * BLOCK, BLOCK)
                bias = bias_ref[0, jnp.where(blk == 0, 1, 0)]
            else:
                k0 = pl.multiple_of((blk - 1) * BLOCK, BLOCK)
                bias = bias_ref[0, 0]
            store(0, pl.ds(q0, BLOCK),
                  unit(q1_ref[0, pl.ds(q0, BLOCK), :], k1_ref[0, pl.ds(k0, 2 * BLOCK), :],
                       v1_ref[0, pl.ds(k0, 2 * BLOCK), :], bias))
        return carry

    lax.fori_loop(0, seq // BLOCK // unroll, body1, 0)

    def body4(g, carry):
        for i in range(unroll):
            r = g * (unroll // nb4) + i // nb4
            blk = i % nb4
            k0 = max(blk - 1, 0) * BLOCK
            store(1, pl.ds(r + 4 * blk * BLOCK, BLOCK, stride=4),
                  unit(q4_ref[0, r, pl.ds(blk * BLOCK, BLOCK), :],
                       k4_ref[0, r, pl.ds(k0, 2 * BLOCK), :],
                       v4_ref[0, r, pl.ds(k0, 2 * BLOCK), :], bias_ref[1, 1 if blk == 0 else 0]))
        return carry

    lax.fori_loop(0, 4 * nb4 // unroll, body4, 0)

    def body16(g, carry):
        for i in range(unroll):
            r = g * unroll + i
            store(2, pl.ds(r, BLOCK, stride=16),
                  unit(q16_ref[0, r], k16_ref[0, r], v16_ref[0, r], bias16_ref[...]))
        return carry

    lax.fori_loop(0, 16 // unroll, body16, 0)

    def merge(blk, carry):
        rows = pl.ds(pl.multiple_of(blk * (2 * BLOCK), 2 * BLOCK), 2 * BLOCK)
        m0, m1, m2 = m_ref[0, rows, :], m_ref[1, rows, :], m_ref[2, rows, :]
        mx = jnp.maximum(jnp.maximum(m0, m1), m2)
        w0, w1, w2 = jnp.exp(m0 - mx), jnp.exp(m1 - mx), jnp.exp(m2 - mx)
        num = w0 * acc_ref[0, rows, :] + w1 * acc_ref[1, rows, :] + w2 * acc_ref[2, rows, :]
        den = w0 * l_ref[0, rows, :] + w1 * l_ref[1, rows, :] + w2 * l_ref[2, rows, :]
        o_ref[0, rows, :] = num / den
        return carry

    lax.fori_loop(0, seq // (2 * BLOCK), merge, 0)


def _attn_a(slopes, nat, a4, a16):
    bsz, seq, _ = nat.shape
    pairs = A_WIDTH // LANES

    def col(base):
        return lambda b, hp: (b, 0, base + hp)

    def col4(base):
        return lambda b, hp: (b, 0, 0, base + hp)

    nat_spec = [pl.BlockSpec((1, seq, LANES), col(i * pairs)) for i in range(3)]
    a4_spec = [pl.BlockSpec((1, 4, seq // 4, LANES), col4(i * pairs)) for i in range(3)]
    a16_spec = [pl.BlockSpec((1, 16, seq // 16, LANES), col4(i * pairs)) for i in range(3)]
    return pl.pallas_call(
        _attn_a_kernel,
        grid=(bsz, pairs),
        in_specs=[pl.BlockSpec(memory_space=pltpu.SMEM)] + nat_spec + a4_spec + a16_spec,
        out_specs=pl.BlockSpec((1, seq, LANES), lambda b, hp: (b, 0, hp)),
        out_shape=jax.ShapeDtypeStruct((bsz, seq, A_WIDTH), F32),
        scratch_shapes=[pltpu.VMEM((3, seq, LANES), F32)] * 3 + [
            pltpu.VMEM((2, 2, 2 * BLOCK, 2 * BLOCK), F32), pltpu.VMEM((2 * BLOCK, BLOCK), F32)],
        compiler_params=_params("parallel", "parallel"),
        name="attn_a",
    )(slopes, nat, nat, nat, a4, a4, a4, a16, a16, a16)


def _attn_b_kernel(slopes_ref, sinks_ref, q_ref, k_ref, v_ref, o_ref, kd_ref, vd_ref, bias_ref):
    g = pl.program_id(1)
    seq = q_ref.shape[1]
    lane = lax.broadcasted_iota(jnp.int32, (1, LANES), 1)
    head0 = lane < HEAD_DIM
    mine = (lane >= HEAD_DIM).astype(jnp.int32) == g

    def both_halves(ref):
        x = ref[0].astype(F32)
        return jnp.where(mine, x, pltpu.roll(x, HEAD_DIM, 1)).astype(BF16)

    kd_ref[...] = both_halves(k_ref)
    vd_ref[...] = both_halves(v_ref)

    neg_slopes = [-slopes_ref[2 * (g * B_GROUP + j) + 1] for j in range(B_GROUP)]
    sinks = [sinks_ref[g * B_GROUP + j] for j in range(B_GROUP)]

    for lead in (False, True):
        qi = lax.broadcasted_iota(jnp.int32, (BLOCK, 2 * BLOCK), 0)
        kj = lax.broadcasted_iota(jnp.int32, (BLOCK, 2 * BLOCK), 1)
        rel = qi - kj if lead else qi + BLOCK - kj
        valid = (rel >= 0) & (rel <= B_WINDOW)
        rel_f = rel.astype(F32)
        for j in range(B_GROUP):
            bias_ref[int(lead), pl.ds(j * BLOCK, BLOCK), :] = jnp.where(valid, rel_f * neg_slopes[j], NEG_INF)

    def unit(blk, lead_possible):
        q0 = pl.multiple_of(blk * BLOCK, BLOCK)
        if lead_possible:
            k0 = pl.multiple_of(jnp.maximum(blk - 1, 0) * BLOCK, BLOCK)
            bias = bias_ref[jnp.where(blk == 0, 1, 0)]
        else:
            k0 = pl.multiple_of((blk - 1) * BLOCK, BLOCK)
            bias = bias_ref[0]
        k = kd_ref[pl.ds(k0, 2 * BLOCK), :]
        v = vd_ref[pl.ds(k0, 2 * BLOCK), :]
        q = q_ref[0, pl.ds(q0, BLOCK), :] * QK_SCALE
        stacked = []
        for j in range(B_GROUP):
            qp = q[:, (j // 2) * LANES:(j // 2 + 1) * LANES]
            stacked.append(jnp.where(head0 if j % 2 == 0 else ~head0, qp, jnp.zeros_like(qp)))
        s_all = _dot_nt(jnp.concatenate(stacked, axis=0), k) + bias
        probs, inv = [], []
        for j in range(B_GROUP):
            s = s_all[j * BLOCK:(j + 1) * BLOCK]
            m = jnp.maximum(jnp.max(s, axis=-1, keepdims=True), sinks[j])
            e = jnp.exp(s - m)
            inv.append(1.0 / (jnp.sum(e, axis=-1, keepdims=True) + jnp.exp(sinks[j] - m)))
            probs.append(e.astype(BF16))
        pv = _dot(jnp.concatenate(probs, axis=0), v)
        for p in range(B_GROUP // 2):
            even = pv[(2 * p) * BLOCK:(2 * p + 1) * BLOCK]
            odd = pv[(2 * p + 1) * BLOCK:(2 * p + 2) * BLOCK]
            o_ref[0, pl.ds(q0, BLOCK), p * LANES:(p + 1) * LANES] = (
                jnp.where(head0, even, odd) * jnp.where(head0, inv[2 * p], inv[2 * p + 1]))

    def body(it, carry):
        for i in range(ATTN_B_UNITS_PER_ITER):
            unit(it * ATTN_B_UNITS_PER_ITER + i, i == 0)
        return carry

    lax.fori_loop(0, seq // BLOCK // ATTN_B_UNITS_PER_ITER, body, 0)


def _attn_b(slopes, sinks, zb):
    bsz, seq, _ = zb.shape
    qw = B_GROUP * HEAD_DIM
    k_blk = B_WIDTH // LANES
    return pl.pallas_call(
        _attn_b_kernel,
        grid=(bsz, B_KV_HEADS),
        in_specs=[
            pl.BlockSpec(memory_space=pltpu.SMEM),
            pl.BlockSpec(memory_space=pltpu.SMEM),
            pl.BlockSpec((1, seq, qw), lambda b, g: (b, 0, g)),
            pl.BlockSpec((1, seq, LANES), lambda b, g: (b, 0, k_blk)),
            pl.BlockSpec((1, seq, LANES), lambda b, g: (b, 0, k_blk + 1)),
        ],
        out_specs=pl.BlockSpec((1, seq, qw), lambda b, g: (b, 0, g)),
        out_shape=jax.ShapeDtypeStruct((bsz, seq, B_WIDTH), F32),
        scratch_shapes=[pltpu.VMEM((seq, LANES), BF16)] * 2 + [
            pltpu.VMEM((2, B_GROUP * BLOCK, 2 * BLOCK), F32)],
        compiler_params=_params("parallel", "parallel"),
        name="attn_b",
    )(slopes, sinks, zb, zb, zb)


def _out_kernel(h_ref, oa_ref, ob_ref, ga_ref, gb_ref, wo_ref, bo_ref, gpost_ref, o_ref):
    na = _rms(oa_ref[...], ga_ref[...]).astype(BF16)
    nb = _rms(ob_ref[...], gb_ref[...]).astype(BF16)
    o = _dot(na, wo_ref[0:A_WIDTH, :]) + _dot(nb, wo_ref[A_WIDTH:A_WIDTH + B_WIDTH, :]) + bo_ref[...]
    o_ref[...] = h_ref[...] + _rms(o, gpost_ref[...])


def _out_proj(h, oa, ob, g_a, g_b, w_o, b_o, g_post):
    t, d = h.shape
    tm = OUT_TM
    row = lambda i: (i, 0)
    return pl.pallas_call(
        _out_kernel,
        grid=(t // tm,),
        in_specs=[
            pl.BlockSpec((tm, d), row),
            pl.BlockSpec((tm, A_WIDTH), row),
            pl.BlockSpec((tm, B_WIDTH), row),
            _resident((1, A_WIDTH)),
            _resident((1, B_WIDTH)),
            _resident((A_WIDTH + B_WIDTH, d)),
            _resident((1, d)),
            _resident((1, d)),
        ],
        out_specs=pl.BlockSpec((tm, d), row),
        out_shape=jax.ShapeDtypeStruct((t, d), F32),
        compiler_params=_params("parallel"),
        name="out_proj",
    )(h, oa, ob, g_a, g_b, w_o, b_o, g_post)


def _ple_kernel(h_ref, p_ref, gpre_ref, wg_ref, wp_ref, gpost_ref, o_ref):
    h = h_ref[...]
    gate = jax.nn.sigmoid(_dot(_rms(h, gpre_ref[...]).astype(BF16), wg_ref[...]))
    e = _dot(p_ref[...].astype(BF16), wp_ref[...])
    o_ref[...] = h + _rms(gate * e, gpost_ref[...])


def _ple(h, p, g_pre, w_gate, w_proj, g_post):
    t, d = h.shape
    pd = p.shape[1]
    tm = PLE_TM
    row = lambda i: (i, 0)
    return pl.pallas_call(
        _ple_kernel,
        grid=(t // tm,),
        in_specs=[
            pl.BlockSpec((tm, d), row),
            pl.BlockSpec((tm, pd), row),
            _resident((1, d)),
            _resident((d, d)),
            _resident((pd, d)),
            _resident((1, d)),
        ],
        out_specs=pl.BlockSpec((tm, d), row),
        out_shape=jax.ShapeDtypeStruct((t, d), F32),
        compiler_params=_params("parallel"),
        name="ple",
    )(h, p, g_pre, w_gate, w_proj, g_post)


def _alibi_slopes(n):
    return jnp.exp2(-ALIBI_MAX_BIAS * (jnp.arange(n, dtype=F32) + 1.0) / n)


def kernel(x, p, g_ffn1_pre, w_ffn1_gate, w_ffn1_up, w_ffn1_down, g_ffn1_post, g_mix_pre, w_qkv, b_qkv, attn_sinks, g_out_a, g_out_b, w_o, b_o, g_mix_post, g_ffn2_pre, w_ffn2_gate, w_ffn2_up, w_ffn2_down, g_ffn2_post, g_ple_pre, w_ple_gate, w_ple_proj, g_ple_post):
    bsz, seq, d = x.shape
    depth = p.shape[0]
    t = bsz * seq
    slopes = _alibi_slopes(A_HEADS + B_Q_HEADS)
    bf = lambda w: w.astype(BF16)
    h = x.reshape(t, d)
    for i in range(depth):
        h = _ffn(h, g_ffn1_pre[i][None], bf(w_ffn1_gate[i]), bf(w_ffn1_up[i]), bf(w_ffn1_down[i]),
                 g_ffn1_post[i][None])
        wq = bf(w_qkv[i])
        nat, a4, a16, zb = _qkv(h, g_mix_pre[i][None], wq[:, :3 * A_WIDTH], b_qkv[i][None, :3 * A_WIDTH],
                                wq[:, 3 * A_WIDTH:], b_qkv[i][None, 3 * A_WIDTH:], bsz, seq)
        oa = _attn_a(slopes, nat.reshape(bsz, seq, 3 * A_WIDTH), a4, a16)
        ob = _attn_b(slopes, attn_sinks[i].reshape(-1), zb.reshape(bsz, seq, ZB_WIDTH))
        h = _out_proj(h, oa.reshape(t, A_WIDTH), ob.reshape(t, B_WIDTH), g_out_a[i][None], g_out_b[i][None],
                      bf(w_o[i]), b_o[i][None], g_mix_post[i][None])
        h = _ffn(h, g_ffn2_pre[i][None], bf(w_ffn2_gate[i]), bf(w_ffn2_up[i]), bf(w_ffn2_down[i]),
                 g_ffn2_post[i][None])
        h = _ple(h, p[i].reshape(t, -1), g_ple_pre[i][None], bf(w_ple_gate[i]), bf(w_ple_proj[i]),
                 g_ple_post[i][None])
    return h.reshape(bsz, seq, d)
```

```python
import functools

import jax
import jax.numpy as jnp
from jax import lax
from jax.experimental import pallas as pl
from jax.experimental.pallas import tpu as pltpu

F32 = jnp.float32
BF16 = jnp.bfloat16

HEAD_DIM = 64
LANES = 128
MXU_WIDTH = 256
BLOCK = 128
A_HEADS = 16
B_Q_HEADS = 16
B_KV_HEADS = 2
B_GROUP = B_Q_HEADS // B_KV_HEADS
A_WIDTH = A_HEADS * HEAD_DIM
B_WIDTH = B_Q_HEADS * HEAD_DIM
KV_B_WIDTH = B_KV_HEADS * HEAD_DIM
ZB_WIDTH = B_WIDTH + 2 * KV_B_WIDTH
DILATIONS = (1, 4, 16)
A_WINDOW = 128
B_WINDOW = 127
ALIBI_MAX_BIAS = 8.0
EPS = 1e-6
QK_SCALE = HEAD_DIM ** -0.5
NEG_INF = float("-inf")

VMEM_LIMIT = 61 * 1024 * 1024

NORM_ROWS = 256
FFN_TM = 1024
FFN_TF = 512
FFN_SPLIT = 2
QKV_TM = 512
ATTN_A_UNITS_PER_ITER = 16
ATTN_B_UNITS_PER_ITER = 4
OUT_TM = 512
PLE_TM = 512


def _rms(x, g):
    return x * lax.rsqrt(jnp.mean(x * x, axis=-1, keepdims=True) + EPS) * g


def _dot(a, b):
    return jnp.dot(a, b, preferred_element_type=F32)


def _dot_nt(a, b):
    return lax.dot_general(a, b, (((1,), (1,)), ((), ())), preferred_element_type=F32)


def _resident(shape):
    return pl.BlockSpec(shape, lambda *_: (0,) * len(shape), pipeline_mode=pl.Buffered(1))


def _params(*semantics):
    return pltpu.CompilerParams(dimension_semantics=semantics, vmem_limit_bytes=VMEM_LIMIT)


def _ffn_kernel(x_ref, gpre_ref, wg_ref, wu_ref, wd_ref, gpost_ref, o_ref, xn_ref):
    j = pl.program_id(1)
    last = pl.num_programs(1) - 1
    tm = x_ref.shape[0]
    halves = [pl.ds(i * (tm // FFN_SPLIT), tm // FFN_SPLIT) for i in range(FFN_SPLIT)]

    def prenorm(rows):
        for c in range(rows.size // NORM_ROWS):
            r = pl.ds(rows.start + c * NORM_ROWS, NORM_ROWS)
            xn_ref[r, :] = _rms(x_ref[r, :], gpre_ref[...]).astype(BF16)

    def postnorm(rows):
        for c in range(rows.size // NORM_ROWS):
            r = pl.ds(rows.start + c * NORM_ROWS, NORM_ROWS)
            o_ref[r, :] = x_ref[r, :] + 0.5 * _rms(o_ref[r, :], gpost_ref[...])

    def swiglu(rows, accumulate):
        xn = xn_ref[rows, :]
        gate = _dot(xn, wg_ref[...])
        up = _dot(xn, wu_ref[...])
        down = _dot((gate * jax.nn.sigmoid(gate) * up).astype(BF16), wd_ref[...])
        o_ref[rows, :] = o_ref[rows, :] + down if accumulate else down

    @pl.when(j == 0)
    def _():
        for rows in halves:
            prenorm(rows)
        for rows in halves:
            swiglu(rows, False)

    @pl.when((j > 0) & (j < last))
    def _():
        swiglu(pl.ds(0, tm), True)

    @pl.when(j == last)
    def _():
        for rows in halves:
            swiglu(rows, True)
        for rows in halves:
            postnorm(rows)


def _ffn(h, g_pre, w_gate, w_up, w_down, g_post):
    t, d = h.shape
    d_ff = w_gate.shape[1]
    tm, tf = FFN_TM, FFN_TF
    return pl.pallas_call(
        _ffn_kernel,
        grid=(t // tm, d_ff // tf),
        in_specs=[
            pl.BlockSpec((tm, d), lambda i, j: (i, 0)),
            _resident((1, d)),
            pl.BlockSpec((d, tf), lambda i, j: (0, j)),
            pl.BlockSpec((d, tf), lambda i, j: (0, j)),
            pl.BlockSpec((tf, d), lambda i, j: (j, 0)),
            _resident((1, d)),
        ],
        out_specs=pl.BlockSpec((tm, d), lambda i, j: (i, 0)),
        out_shape=jax.ShapeDtypeStruct((t, d), F32),
        scratch_shapes=[pltpu.VMEM((tm, d), BF16)],
        compiler_params=_params("parallel", "arbitrary"),
        name="ffn",
    )(h, g_pre, w_gate, w_up, w_down, g_post)


def _qkv_kernel(h_ref, g_ref, wa_ref, ba_ref, wb_ref, bb_ref,
                nat_ref, a4_ref, a16_ref, zb_ref, zs_ref, z4_ref):
    tm = h_ref.shape[0]
    xn = _rms(h_ref[...], g_ref[...]).astype(BF16)
    zb_ref[...] = (_dot(xn, wb_ref[...]) + bb_ref[...]).astype(BF16)
    for c2 in range(wa_ref.shape[1] // MXU_WIDTH):
        wide = slice(c2 * MXU_WIDTH, (c2 + 1) * MXU_WIDTH)
        z2 = _dot(xn, wa_ref[:, wide]) + ba_ref[:, wide]
        for half in range(MXU_WIDTH // LANES):
            c = c2 * (MXU_WIDTH // LANES) + half
            zc = z2[:, half * LANES:(half + 1) * LANES]
            zs_ref[c] = zc
            nat_ref[c] = zc.astype(BF16)

    def copy(c, carry):
        for r in range(4):
            z4 = zs_ref[c, pl.ds(r, tm // 4, stride=4), :]
            a4_ref[0, c, r] = z4.astype(BF16)
            z4_ref[r] = z4
        for r in range(16):
            a16_ref[0, c, r] = z4_ref[r % 4, pl.ds(r // 4, tm // 16, stride=4), :].astype(BF16)
        return carry

    lax.fori_loop(0, zs_ref.shape[0], copy, 0)


def _qkv(h, g, w_a, b_a, w_b, b_b, bsz, seq):
    t, d = h.shape
    tm = QKV_TM
    tn = 3 * A_WIDTH
    nc = tn // LANES
    per_seq = seq // tm
    return pl.pallas_call(
        _qkv_kernel,
        grid=(t // tm,),
        in_specs=[
            pl.BlockSpec((tm, d), lambda i: (i, 0)),
            _resident((1, d)),
            _resident((d, tn)),
            _resident((1, tn)),
            _resident((d, ZB_WIDTH)),
            _resident((1, ZB_WIDTH)),
        ],
        out_specs=[
            pl.BlockSpec((nc, tm, LANES), lambda i: (0, i, 0)),
            pl.BlockSpec((1, nc, 4, tm // 4, LANES), lambda i: (i // per_seq, 0, 0, i % per_seq, 0)),
            pl.BlockSpec((1, nc, 16, tm // 16, LANES), lambda i: (i // per_seq, 0, 0, i % per_seq, 0)),
            pl.BlockSpec((tm, ZB_WIDTH), lambda i: (i, 0)),
        ],
        out_shape=[
            jax.ShapeDtypeStruct((nc, t, LANES), BF16),
            jax.ShapeDtypeStruct((bsz, nc, 4, seq // 4, LANES), BF16),
            jax.ShapeDtypeStruct((bsz, nc, 16, seq // 16, LANES), BF16),
            jax.ShapeDtypeStruct((t, ZB_WIDTH), BF16),
        ],
        scratch_shapes=[pltpu.VMEM((nc, tm, LANES), F32), pltpu.VMEM((4, tm // 4, LANES), F32)],
        compiler_params=_params("parallel"),
        name="qkv",
    )(h, g, w_a, b_a, w_b, b_b)


def _attn_a_kernel(slopes_ref,
                   q1_ref, k1_ref, v1_ref, q4_ref, k4_ref, v4_ref, q16_ref, k16_ref, v16_ref,
                   o_ref, acc_ref, m_ref, l_ref, bias_ref, bias16_ref):
    hp = pl.program_id(1)
    seq = q1_ref.shape[1]
    lane = lax.broadcasted_iota(jnp.int32, (1, LANES), 1)
    head0 = lane < HEAD_DIM
    neg_slopes = (-slopes_ref[4 * hp], -slopes_ref[4 * hp + 2])

    def bias_table(nk, lead, dil):
        qi = lax.broadcasted_iota(jnp.int32, (BLOCK, nk), 0)
        kj = lax.broadcasted_iota(jnp.int32, (BLOCK, nk), 1)
        rel = qi - kj if lead else qi + (nk - BLOCK) - kj
        valid = (rel >= 0) & (rel <= A_WINDOW)
        rel_f = rel.astype(F32)
        return jnp.concatenate(
            [jnp.where(valid, rel_f * (ns * float(dil)), NEG_INF) for ns in neg_slopes], axis=0)

    for bi, dil in enumerate(DILATIONS[:2]):
        bias_ref[bi, 0] = bias_table(2 * BLOCK, False, dil)
        bias_ref[bi, 1] = bias_table(2 * BLOCK, True, dil)
    bias16_ref[...] = bias_table(BLOCK, True, DILATIONS[2])

    def unit(q, k, v, bias):
        zero = jnp.zeros_like(q)
        qs = jnp.concatenate([jnp.where(head0, q, zero), jnp.where(head0, zero, q)], axis=0)
        s = _dot_nt(qs * QK_SCALE, k) + bias
        m = jnp.max(s, axis=-1, keepdims=True)
        e = jnp.exp(s - m)
        l = jnp.sum(e, axis=-1, keepdims=True)
        pv = _dot(e.astype(BF16), v)
        return (jnp.where(head0, pv[:BLOCK], pv[BLOCK:]),
                jnp.where(head0, m[:BLOCK], m[BLOCK:]),
                jnp.where(head0, l[:BLOCK], l[BLOCK:]))

    def store(branch, rows, res):
        acc_ref[branch, rows, :] = res[0]
        m_ref[branch, rows, :] = res[1]
        l_ref[branch, rows, :] = res[2]

    unroll = ATTN_A_UNITS_PER_ITER
    nb4 = seq // 4 // BLOCK

    def body1(g, carry):
        for i in range(unroll):
            blk = g * unroll + i
            q0 = pl.multiple_of(blk * BLOCK, BLOCK)
            if i == 0:
                k0 = pl.multiple_of(jnp.maximum(blk - 1, 0) * BLOCK, BLOCK)
                bias = bias_ref[0, jnp.where(blk == 0, 1, 0)]
            else:
                k0 = pl.multiple_of((blk - 1) * BLOCK, BLOCK)
                bias = bias_ref[0, 0]
            store(0, pl.ds(q0, BLOCK),
                  unit(q1_ref[0, pl.ds(q0, BLOCK), :], k1_ref[0, pl.ds(k0, 2 * BLOCK), :],
                       v1_ref[0, pl.ds(k0, 2 * BLOCK), :], bias))
        return carry

    lax.fori_loop(0, seq // BLOCK // unroll, body1, 0)

    def body4(g, carry):
        for i in range(unroll):
            r = g * (unroll // nb4) + i // nb4
            blk = i % nb4
            k0 = max(blk - 1, 0) * BLOCK
            store(1, pl.ds(r + 4 * blk * BLOCK, BLOCK, stride=4),
                  unit(q4_ref[0, r, pl.ds(blk * BLOCK, BLOCK), :],
                       k4_ref[0, r, pl.ds(k0, 2 * BLOCK), :],
                       v4_ref[0, r, pl.ds(k0, 2 * BLOCK), :], bias_ref[1, 1 if blk == 0 else 0]))
        return carry

    lax.fori_loop(0, 4 * nb4 // unroll, body4, 0)

    def body16(g, carry):
        for i in range(unroll):
            r = g * unroll + i
            store(2, pl.ds(r, BLOCK, stride=16),
                  unit(q16_ref[0, r], k16_ref[0, r], v16_ref[0, r], bias16_ref[...]))
        return carry

    lax.fori_loop(0, 16 // unroll, body16, 0)

    def merge(blk, carry):
        rows = pl.ds(pl.multiple_of(blk * (2 * BLOCK), 2 * BLOCK), 2 * BLOCK)
        m0, m1, m2 = m_ref[0, rows, :], m_ref[1, rows, :], m_ref[2, rows, :]
        mx = jnp.maximum(jnp.maximum(m0, m1), m2)
        w0, w1, w2 = jnp.exp(m0 - mx), jnp.exp(m1 - mx), jnp.exp(m2 - mx)
        num = w0 * acc_ref[0, rows, :] + w1 * acc_ref[1, rows, :] + w2 * acc_ref[2, rows, :]
        den = w0 * l_ref[0, rows, :] + w1 * l_ref[1, rows, :] + w2 * l_ref[2, rows, :]
        o_ref[0, rows, :] = num / den
        return carry

    lax.fori_loop(0, seq // (2 * BLOCK), merge, 0)


def _attn_a(slopes, nat, a4, a16):
    _, bsz, seq, _ = nat.shape
    pairs = A_WIDTH // LANES

    def col(base):
        return lambda b, hp: (base + hp, b, 0, 0)

    def col4(base):
        return lambda b, hp: (b, base + hp, 0, 0, 0)

    nat_spec = [pl.BlockSpec((None, 1, seq, LANES), col(i * pairs)) for i in range(3)]
    a4_spec = [pl.BlockSpec((1, None, 4, seq // 4, LANES), col4(i * pairs)) for i in range(3)]
    a16_spec = [pl.BlockSpec((1, None, 16, seq // 16, LANES), col4(i * pairs)) for i in range(3)]
    return pl.pallas_call(
        _attn_a_kernel,
        grid=(bsz, pairs),
        in_specs=[pl.BlockSpec(memory_space=pltpu.SMEM)] + nat_spec + a4_spec + a16_spec,
        out_specs=pl.BlockSpec((1, seq, LANES), lambda b, hp: (b, 0, hp)),
        out_shape=jax.ShapeDtypeStruct((bsz, seq, A_WIDTH), F32),
        scratch_shapes=[pltpu.VMEM((3, seq, LANES), F32)] * 3 + [
            pltpu.VMEM((2, 2, 2 * BLOCK, 2 * BLOCK), F32), pltpu.VMEM((2 * BLOCK, BLOCK), F32)],
        compiler_params=_params("parallel", "parallel"),
        name="attn_a",
    )(slopes, nat, nat, nat, a4, a4, a4, a16, a16, a16)


def _attn_b_kernel(slopes_ref, sinks_ref, q_ref, k_ref, v_ref, o_ref, kd_ref, vd_ref, bias_ref):
    g = pl.program_id(1)
    seq = q_ref.shape[1]
    lane = lax.broadcasted_iota(jnp.int32, (1, LANES), 1)
    head0 = lane < HEAD_DIM
    mine = (lane >= HEAD_DIM).astype(jnp.int32) == g

    def both_halves(ref):
        x = ref[0].astype(F32)
        return jnp.where(mine, x, pltpu.roll(x, HEAD_DIM, 1)).astype(BF16)

    kd_ref[...] = both_halves(k_ref)
    vd_ref[...] = both_halves(v_ref)

    neg_slopes = [-slopes_ref[2 * (g * B_GROUP + j) + 1] for j in range(B_GROUP)]
    sinks = [sinks_ref[g * B_GROUP + j] for j in range(B_GROUP)]

    for lead in (False, True):
        qi = lax.broadcasted_iota(jnp.int32, (BLOCK, 2 * BLOCK), 0)
        kj = lax.broadcasted_iota(jnp.int32, (BLOCK, 2 * BLOCK), 1)
        rel = qi - kj if lead else qi + BLOCK - kj
        valid = (rel >= 0) & (rel <= B_WINDOW)
        rel_f = rel.astype(F32)
        for j in range(B_GROUP):
            bias_ref[int(lead), pl.ds(j * BLOCK, BLOCK), :] = jnp.where(valid, rel_f * neg_slopes[j], NEG_INF)

    def unit(blk, lead_possible):
        q0 = pl.multiple_of(blk * BLOCK, BLOCK)
        if lead_possible:
            k0 = pl.multiple_of(jnp.maximum(blk - 1, 0) * BLOCK, BLOCK)
            bias = bias_ref[jnp.where(blk == 0, 1, 0)]
        else:
            k0 = pl.multiple_of((blk - 1) * BLOCK, BLOCK)
            bias = bias_ref[0]
        k = kd_ref[pl.ds(k0, 2 * BLOCK), :]
        v = vd_ref[pl.ds(k0, 2 * BLOCK), :]
        q = q_ref[0, pl.ds(q0, BLOCK), :] * QK_SCALE
        stacked = []
        for j in range(B_GROUP):
            qp = q[:, (j // 2) * LANES:(j // 2 + 1) * LANES]
            stacked.append(jnp.where(head0 if j % 2 == 0 else ~head0, qp, jnp.zeros_like(qp)))
        s_all = _dot_nt(jnp.concatenate(stacked, axis=0), k) + bias
        probs, inv = [], []
        for j in range(B_GROUP):
            s = s_all[j * BLOCK:(j + 1) * BLOCK]
            m = jnp.maximum(jnp.max(s, axis=-1, keepdims=True), sinks[j])
            e = jnp.exp(s - m)
            inv.append(1.0 / (jnp.sum(e, axis=-1, keepdims=True) + jnp.exp(sinks[j] - m)))
            probs.append(e.astype(BF16))
        pv = _dot(jnp.concatenate(probs, axis=0), v)
        for p in range(B_GROUP // 2):
            even = pv[(2 * p) * BLOCK:(2 * p + 1) * BLOCK]
            odd = pv[(2 * p + 1) * BLOCK:(2 * p + 2) * BLOCK]
            o_ref[0, pl.ds(q0, BLOCK), p * LANES:(p + 1) * LANES] = (
                jnp.where(head0, even, odd) * jnp.where(head0, inv[2 * p], inv[2 * p + 1]))

    def body(it, carry):
        for i in range(ATTN_B_UNITS_PER_ITER):
            unit(it * ATTN_B_UNITS_PER_ITER + i, i == 0)
        return carry

    lax.fori_loop(0, seq // BLOCK // ATTN_B_UNITS_PER_ITER, body, 0)


def _attn_b(slopes, sinks, zb):
    bsz, seq, _ = zb.shape
    qw = B_GROUP * HEAD_DIM
    k_blk = B_WIDTH // LANES
    return pl.pallas_call(
        _attn_b_kernel,
        grid=(bsz, B_KV_HEADS),
        in_specs=[
            pl.BlockSpec(memory_space=pltpu.SMEM),
            pl.BlockSpec(memory_space=pltpu.SMEM),
            pl.BlockSpec((1, seq, qw), lambda b, g: (b, 0, g)),
            pl.BlockSpec((1, seq, LANES), lambda b, g: (b, 0, k_blk)),
            pl.BlockSpec((1, seq, LANES), lambda b, g: (b, 0, k_blk + 1)),
        ],
        out_specs=pl.BlockSpec((1, seq, qw), lambda b, g: (b, 0, g)),
        out_shape=jax.ShapeDtypeStruct((bsz, seq, B_WIDTH), F32),
        scratch_shapes=[pltpu.VMEM((seq, LANES), BF16)] * 2 + [
            pltpu.VMEM((2, B_GROUP * BLOCK, 2 * BLOCK), F32)],
        compiler_params=_params("parallel", "parallel"),
        name="attn_b",
    )(slopes, sinks, zb, zb, zb)


def _out_kernel(h_ref, oa_ref, ob_ref, ga_ref, gb_ref, wo_ref, bo_ref, gpost_ref, o_ref):
    na = _rms(oa_ref[...], ga_ref[...]).astype(BF16)
    nb = _rms(ob_ref[...], gb_ref[...]).astype(BF16)
    o = _dot(na, wo_ref[0:A_WIDTH, :]) + _dot(nb, wo_ref[A_WIDTH:A_WIDTH + B_WIDTH, :]) + bo_ref[...]
    o_ref[...] = h_ref[...] + _rms(o, gpost_ref[...])


def _out_proj(h, oa, ob, g_a, g_b, w_o, b_o, g_post):
    t, d = h.shape
    tm = OUT_TM
    row = lambda i: (i, 0)
    return pl.pallas_call(
        _out_kernel,
        grid=(t // tm,),
        in_specs=[
            pl.BlockSpec((tm, d), row),
            pl.BlockSpec((tm, A_WIDTH), row),
            pl.BlockSpec((tm, B_WIDTH), row),
            _resident((1, A_WIDTH)),
            _resident((1, B_WIDTH)),
            _resident((A_WIDTH + B_WIDTH, d)),
            _resident((1, d)),
            _resident((1, d)),
        ],
        out_specs=pl.BlockSpec((tm, d), row),
        out_shape=jax.ShapeDtypeStruct((t, d), F32),
        compiler_params=_params("parallel"),
        name="out_proj",
    )(h, oa, ob, g_a, g_b, w_o, b_o, g_post)


def _ple_kernel(h_ref, p_ref, gpre_ref, wg_ref, wp_ref, gpost_ref, o_ref):
    h = h_ref[...]
    gate = jax.nn.sigmoid(_dot(_rms(h, gpre_ref[...]).astype(BF16), wg_ref[...]))
    e = _dot(p_ref[...].astype(BF16), wp_ref[...])
    o_ref[...] = h + _rms(gate * e, gpost_ref[...])


def _ple(h, p, g_pre, w_gate, w_proj, g_post):
    t, d = h.shape
    pd = p.shape[1]
    tm = PLE_TM
    row = lambda i: (i, 0)
    return pl.pallas_call(
        _ple_kernel,
        grid=(t // tm,),
        in_specs=[
            pl.BlockSpec((tm, d), row),
            pl.BlockSpec((tm, pd), row),
            _resident((1, d)),
            _resident((d, d)),
            _resident((pd, d)),
            _resident((1, d)),
        ],
        out_specs=pl.BlockSpec((tm, d), row),
        out_shape=jax.ShapeDtypeStruct((t, d), F32),
        compiler_params=_params("parallel"),
        name="ple",
    )(h, p, g_pre, w_gate, w_proj, g_post)


def _alibi_slopes(n):
    return jnp.exp2(-ALIBI_MAX_BIAS * (jnp.arange(n, dtype=F32) + 1.0) / n)


def kernel(x, p, g_ffn1_pre, w_ffn1_gate, w_ffn1_up, w_ffn1_down, g_ffn1_post, g_mix_pre, w_qkv, b_qkv, attn_sinks, g_out_a, g_out_b, w_o, b_o, g_mix_post, g_ffn2_pre, w_ffn2_gate, w_ffn2_up, w_ffn2_down, g_ffn2_post, g_ple_pre, w_ple_gate, w_ple_proj, g_ple_post):
    bsz, seq, d = x.shape
    depth = p.shape[0]
    t = bsz * seq
    slopes = _alibi_slopes(A_HEADS + B_Q_HEADS)
    bf = lambda w: w.astype(BF16)
    h = x.reshape(t, d)
    for i in range(depth):
        h = _ffn(h, g_ffn1_pre[i][None], bf(w_ffn1_gate[i]), bf(w_ffn1_up[i]), bf(w_ffn1_down[i]),
                 g_ffn1_post[i][None])
        wq = bf(w_qkv[i])
        nat, a4, a16, zb = _qkv(h, g_mix_pre[i][None], wq[:, :3 * A_WIDTH], b_qkv[i][None, :3 * A_WIDTH],
                                wq[:, 3 * A_WIDTH:], b_qkv[i][None, 3 * A_WIDTH:], bsz, seq)
        oa = _attn_a(slopes, nat.reshape(-1, bsz, seq, LANES), a4, a16)
        ob = _attn_b(slopes, attn_sinks[i].reshape(-1), zb.reshape(bsz, seq, ZB_WIDTH))
        h = _out_proj(h, oa.reshape(t, A_WIDTH), ob.reshape(t, B_WIDTH), g_out_a[i][None], g_out_b[i][None],
                      bf(w_o[i]), b_o[i][None], g_mix_post[i][None])
        h = _ffn(h, g_ffn2_pre[i][None], bf(w_ffn2_gate[i]), bf(w_ffn2_up[i]), bf(w_ffn2_down[i]),
                 g_ffn2_post[i][None])
        h = _ple(h, p[i].reshape(t, -1), g_ple_pre[i][None], bf(w_ple_gate[i]), bf(w_ple_proj[i]),
                 g_ple_post[i][None])
    return h.reshape(bsz, seq, d)
```

```python
import functools

import jax
import jax.numpy as jnp
from jax import lax
from jax.experimental import pallas as pl
from jax.experimental.pallas import tpu as pltpu

F32 = jnp.float32
BF16 = jnp.bfloat16

HEAD_DIM = 64
LANES = 128
MXU_WIDTH = 256
BLOCK = 128
A_HEADS = 16
B_Q_HEADS = 16
B_KV_HEADS = 2
B_GROUP = B_Q_HEADS // B_KV_HEADS
A_WIDTH = A_HEADS * HEAD_DIM
B_WIDTH = B_Q_HEADS * HEAD_DIM
KV_B_WIDTH = B_KV_HEADS * HEAD_DIM
ZB_WIDTH = B_WIDTH + 2 * KV_B_WIDTH
DILATIONS = (1, 4, 16)
A_WINDOW = 128
B_WINDOW = 127
ALIBI_MAX_BIAS = 8.0
EPS = 1e-6
QK_SCALE = HEAD_DIM ** -0.5
NEG_INF = float("-inf")

VMEM_LIMIT = 61 * 1024 * 1024

NORM_ROWS = 256
FFN_TM = 1024
FFN_TF = 512
FFN_SPLIT = 2
FFN_HEAD_TF = 256
QKV_TM = 512
ATTN_A_UNITS_PER_ITER = 16
ATTN_B_UNITS_PER_ITER = 4
OUT_TM = 512
PLE_TM = 512


def _rms(x, g):
    return x * lax.rsqrt(jnp.mean(x * x, axis=-1, keepdims=True) + EPS) * g


def _dot(a, b):
    return jnp.dot(a, b, preferred_element_type=F32)


def _dot_nt(a, b):
    return lax.dot_general(a, b, (((1,), (1,)), ((), ())), preferred_element_type=F32)


def _resident(shape):
    return pl.BlockSpec(shape, lambda *_: (0,) * len(shape), pipeline_mode=pl.Buffered(1))


def _params(*semantics):
    return pltpu.CompilerParams(dimension_semantics=semantics, vmem_limit_bytes=VMEM_LIMIT)


def _ffn_steps(x_ref, gpre_ref, weights, gpost_ref, o_ref, xn_ref):
    j = pl.program_id(1)
    last = pl.num_programs(1) - 1
    tm = x_ref.shape[0]
    pieces = [pl.ds(i * (tm // FFN_SPLIT), tm // FFN_SPLIT) for i in range(FFN_SPLIT)]

    def prenorm(rows):
        for c in range(rows.size // NORM_ROWS):
            r = pl.ds(rows.start + c * NORM_ROWS, NORM_ROWS)
            xn_ref[r, :] = _rms(x_ref[r, :], gpre_ref[...]).astype(BF16)

    def postnorm(rows):
        for c in range(rows.size // NORM_ROWS):
            r = pl.ds(rows.start + c * NORM_ROWS, NORM_ROWS)
            o_ref[r, :] = x_ref[r, :] + 0.5 * _rms(o_ref[r, :], gpost_ref[...])

    def swiglu(w, rows, accumulate):
        wg, wu, wd = w
        xn = xn_ref[rows, :]
        gate = _dot(xn, wg)
        up = _dot(xn, wu)
        down = _dot((gate * jax.nn.sigmoid(gate) * up).astype(BF16), wd)
        o_ref[rows, :] = o_ref[rows, :] + down if accumulate else down

    @pl.when(j == 0)
    def _():
        w = weights()
        for rows in pieces:
            prenorm(rows)
        for rows in pieces:
            swiglu(w, rows, False)

    @pl.when((j > 0) & (j < last))
    def _():
        swiglu(weights(), pl.ds(0, tm), True)

    @pl.when(j == last)
    def _():
        w = weights()
        for rows in pieces:
            swiglu(w, rows, True)
        for rows in pieces:
            postnorm(rows)


def _ffn_head_kernel(x_ref, gpre_ref, wg32_ref, wu32_ref, wd32_ref, gpost_ref,
                     o_ref, wg_ref, wu_ref, wd_ref, xn_ref):
    def weights():
        w = [r[...].astype(BF16) for r in (wg32_ref, wu32_ref, wd32_ref)]
        for dst, val in zip((wg_ref, wu_ref, wd_ref), w):
            dst[...] = val
        return w

    _ffn_steps(x_ref, gpre_ref, weights, gpost_ref, o_ref, xn_ref)


def _ffn_main_kernel(x_ref, gpre_ref, wg_ref, wu_ref, wd_ref, gpost_ref, head_out_ref, o_ref, xn_ref):
    del head_out_ref
    _ffn_steps(x_ref, gpre_ref, lambda: (wg_ref[...], wu_ref[...], wd_ref[...]), gpost_ref, o_ref, xn_ref)


def _ffn(h, g_pre, w_gate, w_up, w_down, g_post):
    t, d = h.shape
    d_ff = w_gate.shape[1]
    out_sds = jax.ShapeDtypeStruct((t, d), F32)

    tm, tf = FFN_TM, FFN_HEAD_TF
    head_out, wg, wu, wd = pl.pallas_call(
        _ffn_head_kernel,
        grid=(1, d_ff // tf),
        in_specs=[
            pl.BlockSpec((tm, d), lambda i, j: (0, 0), pipeline_mode=pl.Buffered(1)),
            _resident((1, d)),
            pl.BlockSpec((d, tf), lambda i, j: (0, j)),
            pl.BlockSpec((d, tf), lambda i, j: (0, j)),
            pl.BlockSpec((tf, d), lambda i, j: (j, 0)),
            _resident((1, d)),
        ],
        out_specs=[
            pl.BlockSpec((tm, d), lambda i, j: (0, 0)),
            pl.BlockSpec((d, tf), lambda i, j: (0, j)),
            pl.BlockSpec((d, tf), lambda i, j: (0, j)),
            pl.BlockSpec((tf, d), lambda i, j: (j, 0)),
        ],
        out_shape=[out_sds, jax.ShapeDtypeStruct(w_gate.shape, BF16),
                   jax.ShapeDtypeStruct(w_up.shape, BF16), jax.ShapeDtypeStruct(w_down.shape, BF16)],
        scratch_shapes=[pltpu.VMEM((tm, d), BF16)],
        compiler_params=_params("arbitrary", "arbitrary"),
        name="ffn_head",
    )(h, g_pre, w_gate, w_up, w_down, g_post)

    tf = FFN_TF
    return pl.pallas_call(
        _ffn_main_kernel,
        grid=(t // tm - 1, d_ff // tf),
        in_specs=[
            pl.BlockSpec((tm, d), lambda i, j: (i + 1, 0)),
            _resident((1, d)),
            pl.BlockSpec((d, tf), lambda i, j: (0, j)),
            pl.BlockSpec((d, tf), lambda i, j: (0, j)),
            pl.BlockSpec((tf, d), lambda i, j: (j, 0)),
            _resident((1, d)),
            pl.BlockSpec(memory_space=pl.ANY),
        ],
        out_specs=pl.BlockSpec((tm, d), lambda i, j: (i + 1, 0)),
        out_shape=out_sds,
        input_output_aliases={6: 0},
        scratch_shapes=[pltpu.VMEM((tm, d), BF16)],
        compiler_params=_params("parallel", "arbitrary"),
        name="ffn",
    )(h, g_pre, wg, wu, wd, g_post, head_out)


def _qkv_kernel(h_ref, g_ref, wa_ref, ba_ref, wb_ref, bb_ref,
                nat_ref, a4_ref, a16_ref, zb_ref, zs_ref, z4_ref):
    tm = h_ref.shape[0]
    xn = _rms(h_ref[...], g_ref[...]).astype(BF16)
    zb_ref[...] = (_dot(xn, wb_ref[...]) + bb_ref[...]).astype(BF16)
    for c2 in range(wa_ref.shape[1] // MXU_WIDTH):
        wide = slice(c2 * MXU_WIDTH, (c2 + 1) * MXU_WIDTH)
        z2 = _dot(xn, wa_ref[:, wide]) + ba_ref[:, wide]
        for half in range(MXU_WIDTH // LANES):
            c = c2 * (MXU_WIDTH // LANES) + half
            zc = z2[:, half * LANES:(half + 1) * LANES]
            zs_ref[c] = zc
            nat_ref[c] = zc.astype(BF16)

    def copy(c, carry):
        for r in range(4):
            z4 = zs_ref[c, pl.ds(r, tm // 4, stride=4), :]
            a4_ref[0, c, r] = z4.astype(BF16)
            z4_ref[r] = z4
        for r in range(16):
            a16_ref[0, c, r] = z4_ref[r % 4, pl.ds(r // 4, tm // 16, stride=4), :].astype(BF16)
        return carry

    lax.fori_loop(0, zs_ref.shape[0], copy, 0)


def _qkv(h, g, w_a, b_a, w_b, b_b, bsz, seq):
    t, d = h.shape
    tm = QKV_TM
    tn = 3 * A_WIDTH
    nc = tn // LANES
    per_seq = seq // tm
    return pl.pallas_call(
        _qkv_kernel,
        grid=(t // tm,),
        in_specs=[
            pl.BlockSpec((tm, d), lambda i: (i, 0)),
            _resident((1, d)),
            _resident((d, tn)),
            _resident((1, tn)),
            _resident((d, ZB_WIDTH)),
            _resident((1, ZB_WIDTH)),
        ],
        out_specs=[
            pl.BlockSpec((nc, tm, LANES), lambda i: (0, i, 0)),
            pl.BlockSpec((1, nc, 4, tm // 4, LANES), lambda i: (i // per_seq, 0, 0, i % per_seq, 0)),
            pl.BlockSpec((1, nc, 16, tm // 16, LANES), lambda i: (i // per_seq, 0, 0, i % per_seq, 0)),
            pl.BlockSpec((tm, ZB_WIDTH), lambda i: (i, 0)),
        ],
        out_shape=[
            jax.ShapeDtypeStruct((nc, t, LANES), BF16),
            jax.ShapeDtypeStruct((bsz, nc, 4, seq // 4, LANES), BF16),
            jax.ShapeDtypeStruct((bsz, nc, 16, seq // 16, LANES), BF16),
            jax.ShapeDtypeStruct((t, ZB_WIDTH), BF16),
        ],
        scratch_shapes=[pltpu.VMEM((nc, tm, LANES), F32), pltpu.VMEM((4, tm // 4, LANES), F32)],
        compiler_params=_params("parallel"),
        name="qkv",
    )(h, g, w_a, b_a, w_b, b_b)


def _attn_a_kernel(slopes_ref,
                   q1_ref, k1_ref, v1_ref, q4_ref, k4_ref, v4_ref, q16_ref, k16_ref, v16_ref,
                   o_ref, acc_ref, m_ref, l_ref, bias_ref, bias16_ref):
    hp = pl.program_id(1)
    seq = q1_ref.shape[1]
    lane = lax.broadcasted_iota(jnp.int32, (1, LANES), 1)
    head0 = lane < HEAD_DIM
    neg_slopes = (-slopes_ref[4 * hp], -slopes_ref[4 * hp + 2])

    def bias_table(nk, lead, dil):
        qi = lax.broadcasted_iota(jnp.int32, (BLOCK, nk), 0)
        kj = lax.broadcasted_iota(jnp.int32, (BLOCK, nk), 1)
        rel = qi - kj if lead else qi + (nk - BLOCK) - kj
        valid = (rel >= 0) & (rel <= A_WINDOW)
        rel_f = rel.astype(F32)
        return jnp.concatenate(
            [jnp.where(valid, rel_f * (ns * float(dil)), NEG_INF) for ns in neg_slopes], axis=0)

    for bi, dil in enumerate(DILATIONS[:2]):
        bias_ref[bi, 0] = bias_table(2 * BLOCK, False, dil)
        bias_ref[bi, 1] = bias_table(2 * BLOCK, True, dil)
    bias16_ref[...] = bias_table(BLOCK, True, DILATIONS[2])

    def unit(q, k, v, bias):
        zero = jnp.zeros_like(q)
        qs = jnp.concatenate([jnp.where(head0, q, zero), jnp.where(head0, zero, q)], axis=0)
        s = _dot_nt(qs * QK_SCALE, k) + bias
        m = jnp.max(s, axis=-1, keepdims=True)
        e = jnp.exp(s - m)
        l = jnp.sum(e, axis=-1, keepdims=True)
        pv = _dot(e.astype(BF16), v)
        return (jnp.where(head0, pv[:BLOCK], pv[BLOCK:]),
                jnp.where(head0, m[:BLOCK], m[BLOCK:]),
                jnp.where(head0, l[:BLOCK], l[BLOCK:]))

    def store(branch, rows, res):
        acc_ref[branch, rows, :] = res[0]
        m_ref[branch, rows, :] = res[1]
        l_ref[branch, rows, :] = res[2]

    unroll = ATTN_A_UNITS_PER_ITER
    nb4 = seq // 4 // BLOCK

    def body1(g, carry):
        for i in range(unroll):
            blk = g * unroll + i
            q0 = pl.multiple_of(blk * BLOCK, BLOCK)
            if i == 0:
                k0 = pl.multiple_of(jnp.maximum(blk - 1, 0) * BLOCK, BLOCK)
                bias = bias_ref[0, jnp.where(blk == 0, 1, 0)]
            else:
                k0 = pl.multiple_of((blk - 1) * BLOCK, BLOCK)
                bias = bias_ref[0, 0]
            store(0, pl.ds(q0, BLOCK),
                  unit(q1_ref[0, pl.ds(q0, BLOCK), :], k1_ref[0, pl.ds(k0, 2 * BLOCK), :],
                       v1_ref[0, pl.ds(k0, 2 * BLOCK), :], bias))
        return carry

    lax.fori_loop(0, seq // BLOCK // unroll, body1, 0)

    def body4(g, carry):
        for i in range(unroll):
            r = g * (unroll // nb4) + i // nb4
            blk = i % nb4
            k0 = max(blk - 1, 0) * BLOCK
            store(1, pl.ds(r + 4 * blk * BLOCK, BLOCK, stride=4),
                  unit(q4_ref[0, r, pl.ds(blk * BLOCK, BLOCK), :],
                       k4_ref[0, r, pl.ds(k0, 2 * BLOCK), :],
                       v4_ref[0, r, pl.ds(k0, 2 * BLOCK), :], bias_ref[1, 1 if blk == 0 else 0]))
        return carry

    lax.fori_loop(0, 4 * nb4 // unroll, body4, 0)

    def body16(g, carry):
        for i in range(unroll):
            r = g * unroll + i
            store(2, pl.ds(r, BLOCK, stride=16),
                  unit(q16_ref[0, r], k16_ref[0, r], v16_ref[0, r], bias16_ref[...]))
        return carry

    lax.fori_loop(0, 16 // unroll, body16, 0)

    def merge(blk, carry):
        rows = pl.ds(pl.multiple_of(blk * (2 * BLOCK), 2 * BLOCK), 2 * BLOCK)
        m0, m1, m2 = m_ref[0, rows, :], m_ref[1, rows, :], m_ref[2, rows, :]
        mx = jnp.maximum(jnp.maximum(m0, m1), m2)
        w0, w1, w2 = jnp.exp(m0 - mx), jnp.exp(m1 - mx), jnp.exp(m2 - mx)
        num = w0 * acc_ref[0, rows, :] + w1 * acc_ref[1, rows, :] + w2 * acc_ref[2, rows, :]
        den = w0 * l_ref[0, rows, :] + w1 * l_ref[1, rows, :] + w2 * l_ref[2, rows, :]
        o_ref[0, rows, :] = num / den
        return carry

    lax.fori_loop(0, seq // (2 * BLOCK), merge, 0)


def _attn_a(slopes, nat, a4, a16):
    _, bsz, seq, _ = nat.shape
    pairs = A_WIDTH // LANES

    def col(base):
        return lambda b, hp: (base + hp, b, 0, 0)

    def col4(base):
        return lambda b, hp: (b, base + hp, 0, 0, 0)

    nat_spec = [pl.BlockSpec((None, 1, seq, LANES), col(i * pairs)) for i in range(3)]
    a4_spec = [pl.BlockSpec((1, None, 4, seq // 4, LANES), col4(i * pairs)) for i in range(3)]
    a16_spec = [pl.BlockSpec((1, None, 16, seq // 16, LANES), col4(i * pairs)) for i in range(3)]
    return pl.pallas_call(
        _attn_a_kernel,
        grid=(bsz, pairs),
        in_specs=[pl.BlockSpec(memory_space=pltpu.SMEM)] + nat_spec + a4_spec + a16_spec,
        out_specs=pl.BlockSpec((1, seq, LANES), lambda b, hp: (b, 0, hp)),
        out_shape=jax.ShapeDtypeStruct((bsz, seq, A_WIDTH), F32),
        scratch_shapes=[pltpu.VMEM((3, seq, LANES), F32)] * 3 + [
            pltpu.VMEM((2, 2, 2 * BLOCK, 2 * BLOCK), F32), pltpu.VMEM((2 * BLOCK, BLOCK), F32)],
        compiler_params=_params("parallel", "parallel"),
        name="attn_a",
    )(slopes, nat, nat, nat, a4, a4, a4, a16, a16, a16)


def _attn_b_kernel(slopes_ref, sinks_ref, q_ref, k_ref, v_ref, o_ref, kd_ref, vd_ref, bias_ref):
    g = pl.program_id(1)
    seq = q_ref.shape[1]
    lane = lax.broadcasted_iota(jnp.int32, (1, LANES), 1)
    head0 = lane < HEAD_DIM
    mine = (lane >= HEAD_DIM).astype(jnp.int32) == g

    def both_halves(ref):
        x = ref[0].astype(F32)
        return jnp.where(mine, x, pltpu.roll(x, HEAD_DIM, 1)).astype(BF16)

    kd_ref[...] = both_halves(k_ref)
    vd_ref[...] = both_halves(v_ref)

    neg_slopes = [-slopes_ref[2 * (g * B_GROUP + j) + 1] for j in range(B_GROUP)]
    sinks = [sinks_ref[g * B_GROUP + j] for j in range(B_GROUP)]

    for lead in (False, True):
        qi = lax.broadcasted_iota(jnp.int32, (BLOCK, 2 * BLOCK), 0)
        kj = lax.broadcasted_iota(jnp.int32, (BLOCK, 2 * BLOCK), 1)
        rel = qi - kj if lead else qi + BLOCK - kj
        valid = (rel >= 0) & (rel <= B_WINDOW)
        rel_f = rel.astype(F32)
        for j in range(B_GROUP):
            bias_ref[int(lead), pl.ds(j * BLOCK, BLOCK), :] = jnp.where(valid, rel_f * neg_slopes[j], NEG_INF)

    def unit(blk, lead_possible):
        q0 = pl.multiple_of(blk * BLOCK, BLOCK)
        if lead_possible:
            k0 = pl.multiple_of(jnp.maximum(blk - 1, 0) * BLOCK, BLOCK)
            bias = bias_ref[jnp.where(blk == 0, 1, 0)]
        else:
            k0 = pl.multiple_of((blk - 1) * BLOCK, BLOCK)
            bias = bias_ref[0]
        k = kd_ref[pl.ds(k0, 2 * BLOCK), :]
        v = vd_ref[pl.ds(k0, 2 * BLOCK), :]
        q = q_ref[0, pl.ds(q0, BLOCK), :] * QK_SCALE
        stacked = []
        for j in range(B_GROUP):
            qp = q[:, (j // 2) * LANES:(j // 2 + 1) * LANES]
            stacked.append(jnp.where(head0 if j % 2 == 0 else ~head0, qp, jnp.zeros_like(qp)))
        s_all = _dot_nt(jnp.concatenate(stacked, axis=0), k) + bias
        probs, inv = [], []
        for j in range(B_GROUP):
            s = s_all[j * BLOCK:(j + 1) * BLOCK]
            m = jnp.maximum(jnp.max(s, axis=-1, keepdims=True), sinks[j])
            e = jnp.exp(s - m)
            inv.append(1.0 / (jnp.sum(e, axis=-1, keepdims=True) + jnp.exp(sinks[j] - m)))
            probs.append(e.astype(BF16))
        pv = _dot(jnp.concatenate(probs, axis=0), v)
        for p in range(B_GROUP // 2):
            even = pv[(2 * p) * BLOCK:(2 * p + 1) * BLOCK]
            odd = pv[(2 * p + 1) * BLOCK:(2 * p + 2) * BLOCK]
            o_ref[0, pl.ds(q0, BLOCK), p * LANES:(p + 1) * LANES] = (
                jnp.where(head0, even, odd) * jnp.where(head0, inv[2 * p], inv[2 * p + 1]))

    def body(it, carry):
        for i in range(ATTN_B_UNITS_PER_ITER):
            unit(it * ATTN_B_UNITS_PER_ITER + i, i == 0)
        return carry

    lax.fori_loop(0, seq // BLOCK // ATTN_B_UNITS_PER_ITER, body, 0)


def _attn_b(slopes, sinks, zb):
    bsz, seq, _ = zb.shape
    qw = B_GROUP * HEAD_DIM
    k_blk = B_WIDTH // LANES
    return pl.pallas_call(
        _attn_b_kernel,
        grid=(bsz, B_KV_HEADS),
        in_specs=[
            pl.BlockSpec(memory_space=pltpu.SMEM),
            pl.BlockSpec(memory_space=pltpu.SMEM),
            pl.BlockSpec((1, seq, qw), lambda b, g: (b, 0, g)),
            pl.BlockSpec((1, seq, LANES), lambda b, g: (b, 0, k_blk)),
            pl.BlockSpec((1, seq, LANES), lambda b, g: (b, 0, k_blk + 1)),
        ],
        out_specs=pl.BlockSpec((1, seq, qw), lambda b, g: (b, 0, g)),
        out_shape=jax.ShapeDtypeStruct((bsz, seq, B_WIDTH), F32),
        scratch_shapes=[pltpu.VMEM((seq, LANES), BF16)] * 2 + [
            pltpu.VMEM((2, B_GROUP * BLOCK, 2 * BLOCK), F32)],
        compiler_params=_params("parallel", "parallel"),
        name="attn_b",
    )(slopes, sinks, zb, zb, zb)


def _out_kernel(h_ref, oa_ref, ob_ref, ga_ref, gb_ref, wo_ref, bo_ref, gpost_ref, o_ref):
    na = _rms(oa_ref[...], ga_ref[...]).astype(BF16)
    nb = _rms(ob_ref[...], gb_ref[...]).astype(BF16)
    o = _dot(na, wo_ref[0:A_WIDTH, :]) + _dot(nb, wo_ref[A_WIDTH:A_WIDTH + B_WIDTH, :]) + bo_ref[...]
    o_ref[...] = h_ref[...] + _rms(o, gpost_ref[...])


def _out_proj(h, oa, ob, g_a, g_b, w_o, b_o, g_post):
    t, d = h.shape
    tm = OUT_TM
    row = lambda i: (i, 0)
    return pl.pallas_call(
        _out_kernel,
        grid=(t // tm,),
        in_specs=[
            pl.BlockSpec((tm, d), row),
            pl.BlockSpec((tm, A_WIDTH), row),
            pl.BlockSpec((tm, B_WIDTH), row),
            _resident((1, A_WIDTH)),
            _resident((1, B_WIDTH)),
            _resident((A_WIDTH + B_WIDTH, d)),
            _resident((1, d)),
            _resident((1, d)),
        ],
        out_specs=pl.BlockSpec((tm, d), row),
        out_shape=jax.ShapeDtypeStruct((t, d), F32),
        compiler_params=_params("parallel"),
        name="out_proj",
    )(h, oa, ob, g_a, g_b, w_o, b_o, g_post)


def _ple_kernel(h_ref, p_ref, gpre_ref, wg_ref, wp_ref, gpost_ref, o_ref):
    h = h_ref[...]
    gate = jax.nn.sigmoid(_dot(_rms(h, gpre_ref[...]).astype(BF16), wg_ref[...]))
    e = _dot(p_ref[...].astype(BF16), wp_ref[...])
    o_ref[...] = h + _rms(gate * e, gpost_ref[...])


def _ple(h, p, g_pre, w_gate, w_proj, g_post):
    t, d = h.shape
    pd = p.shape[1]
    tm = PLE_TM
    row = lambda i: (i, 0)
    return pl.pallas_call(
        _ple_kernel,
        grid=(t // tm,),
        in_specs=[
            pl.BlockSpec((tm, d), row),
            pl.BlockSpec((tm, pd), row),
            _resident((1, d)),
            _resident((d, d)),
            _resident((pd, d)),
            _resident((1, d)),
        ],
        out_specs=pl.BlockSpec((tm, d), row),
        out_shape=jax.ShapeDtypeStruct((t, d), F32),
        compiler_params=_params("parallel"),
        name="ple",
    )(h, p, g_pre, w_gate, w_proj, g_post)


def _alibi_slopes(n):
    return jnp.exp2(-ALIBI_MAX_BIAS * (jnp.arange(n, dtype=F32) + 1.0) / n)


def kernel(x, p, g_ffn1_pre, w_ffn1_gate, w_ffn1_up, w_ffn1_down, g_ffn1_post, g_mix_pre, w_qkv, b_qkv, attn_sinks, g_out_a, g_out_b, w_o, b_o, g_mix_post, g_ffn2_pre, w_ffn2_gate, w_ffn2_up, w_ffn2_down, g_ffn2_post, g_ple_pre, w_ple_gate, w_ple_proj, g_ple_post):
    bsz, seq, d = x.shape
    depth = p.shape[0]
    t = bsz * seq
    slopes = _alibi_slopes(A_HEADS + B_Q_HEADS)
    bf = lambda w: w.astype(BF16)
    h = x.reshape(t, d)
    for i in range(depth):
        h = _ffn(h, g_ffn1_pre[i][None], w_ffn1_gate[i], w_ffn1_up[i], w_ffn1_down[i], g_ffn1_post[i][None])
        wq = bf(w_qkv[i])
        nat, a4, a16, zb = _qkv(h, g_mix_pre[i][None], wq[:, :3 * A_WIDTH], b_qkv[i][None, :3 * A_WIDTH],
                                wq[:, 3 * A_WIDTH:], b_qkv[i][None, 3 * A_WIDTH:], bsz, seq)
        oa = _attn_a(slopes, nat.reshape(-1, bsz, seq, LANES), a4, a16)
        ob = _attn_b(slopes, attn_sinks[i].reshape(-1), zb.reshape(bsz, seq, ZB_WIDTH))
        h = _out_proj(h, oa.reshape(t, A_WIDTH), ob.reshape(t, B_WIDTH), g_out_a[i][None], g_out_b[i][None],
                      bf(w_o[i]), b_o[i][None], g_mix_post[i][None])
        h = _ffn(h, g_ffn2_pre[i][None], w_ffn2_gate[i], w_ffn2_up[i], w_ffn2_down[i], g_ffn2_post[i][None])
        h = _ple(h, p[i].reshape(t, -1), g_ple_pre[i][None], bf(w_ple_gate[i]), bf(w_ple_proj[i]),
                 g_ple_post[i][None])
    return h.reshape(bsz, seq, d)
```

```python
import jax
import jax.numpy as jnp
from jax import lax
from jax.experimental import pallas as pl
from jax.experimental.pallas import tpu as pltpu

F32 = jnp.float32
BF16 = jnp.bfloat16

HEAD_DIM = 64
LANES = 128
MXU_WIDTH = 256
BLOCK = 128
A_HEADS = 16
B_Q_HEADS = 16
B_KV_HEADS = 2
B_GROUP = B_Q_HEADS // B_KV_HEADS
A_WIDTH = A_HEADS * HEAD_DIM
B_WIDTH = B_Q_HEADS * HEAD_DIM
KV_B_WIDTH = B_KV_HEADS * HEAD_DIM
ZB_WIDTH = B_WIDTH + 2 * KV_B_WIDTH
DILATIONS = (1, 4, 16)
A_WINDOW = 128
B_WINDOW = 127
ALIBI_MAX_BIAS = 8.0
EPS = 1e-6
QK_SCALE = HEAD_DIM ** -0.5
NEG_INF = float("-inf")

VMEM_LIMIT = 61 * 1024 * 1024

NORM_ROWS = 256
FFN_TM = 1024
FFN_TF = 512
FFN_SPLIT = 2
FFN_HEAD_TF = 256
QKV_TM = 512
ATTN_A_UNITS_PER_ITER = 16
ATTN_B_UNITS_PER_ITER = 8
OUT_TM = 512
PLE_TM = 1024


def _rms(x, g):
    return x * lax.rsqrt(jnp.mean(x * x, axis=-1, keepdims=True) + EPS) * g


def _dot(a, b):
    return jnp.dot(a, b, preferred_element_type=F32)


def _dot_nt(a, b):
    return lax.dot_general(a, b, (((1,), (1,)), ((), ())), preferred_element_type=F32)


def _resident(shape):
    return pl.BlockSpec(shape, lambda *_: (0,) * len(shape), pipeline_mode=pl.Buffered(1))


def _params(*semantics):
    return pltpu.CompilerParams(dimension_semantics=semantics, vmem_limit_bytes=VMEM_LIMIT)


def _ffn_steps(x_ref, gpre_ref, weights, gpost_ref, o_ref, xn_ref):
    j = pl.program_id(1)
    last = pl.num_programs(1) - 1
    tm = x_ref.shape[0]
    pieces = [pl.ds(i * (tm // FFN_SPLIT), tm // FFN_SPLIT) for i in range(FFN_SPLIT)]

    def prenorm(rows):
        for c in range(rows.size // NORM_ROWS):
            r = pl.ds(rows.start + c * NORM_ROWS, NORM_ROWS)
            xn_ref[r, :] = _rms(x_ref[r, :], gpre_ref[...]).astype(BF16)

    def postnorm(rows):
        for c in range(rows.size // NORM_ROWS):
            r = pl.ds(rows.start + c * NORM_ROWS, NORM_ROWS)
            o_ref[r, :] = x_ref[r, :] + 0.5 * _rms(o_ref[r, :], gpost_ref[...])

    def swiglu(w, rows, accumulate):
        wg, wu, wd = w
        xn = xn_ref[rows, :]
        gate = _dot(xn, wg)
        up = _dot(xn, wu)
        down = _dot((gate * jax.nn.sigmoid(gate) * up).astype(BF16), wd)
        o_ref[rows, :] = o_ref[rows, :] + down if accumulate else down

    @pl.when(j == 0)
    def _():
        w = weights()
        for rows in pieces:
            prenorm(rows)
        for rows in pieces:
            swiglu(w, rows, False)

    @pl.when((j > 0) & (j < last))
    def _():
        swiglu(weights(), pl.ds(0, tm), True)

    @pl.when(j == last)
    def _():
        w = weights()
        for rows in pieces:
            swiglu(w, rows, True)
        for rows in pieces:
            postnorm(rows)


def _ffn_head_kernel(x_ref, gpre_ref, wg32_ref, wu32_ref, wd32_ref, gpost_ref,
                     o_ref, wg_ref, wu_ref, wd_ref, xn_ref):
    def weights():
        w = [r[...].astype(BF16) for r in (wg32_ref, wu32_ref, wd32_ref)]
        for dst, val in zip((wg_ref, wu_ref, wd_ref), w):
            dst[...] = val
        return w

    _ffn_steps(x_ref, gpre_ref, weights, gpost_ref, o_ref, xn_ref)


def _ffn_main_kernel(x_ref, gpre_ref, wg_ref, wu_ref, wd_ref, gpost_ref, head_out_ref, o_ref, xn_ref):
    del head_out_ref
    _ffn_steps(x_ref, gpre_ref, lambda: (wg_ref[...], wu_ref[...], wd_ref[...]), gpost_ref, o_ref, xn_ref)


def _ffn(h, g_pre, w_gate, w_up, w_down, g_post):
    t, d = h.shape
    d_ff = w_gate.shape[1]
    out_sds = jax.ShapeDtypeStruct((t, d), F32)

    tm, tf = FFN_TM, FFN_HEAD_TF
    head_out, wg, wu, wd = pl.pallas_call(
        _ffn_head_kernel,
        grid=(1, d_ff // tf),
        in_specs=[
            pl.BlockSpec((tm, d), lambda i, j: (0, 0), pipeline_mode=pl.Buffered(1)),
            _resident((1, d)),
            pl.BlockSpec((d, tf), lambda i, j: (0, j)),
            pl.BlockSpec((d, tf), lambda i, j: (0, j)),
            pl.BlockSpec((tf, d), lambda i, j: (j, 0)),
            _resident((1, d)),
        ],
        out_specs=[
            pl.BlockSpec((tm, d), lambda i, j: (0, 0)),
            pl.BlockSpec((d, tf), lambda i, j: (0, j)),
            pl.BlockSpec((d, tf), lambda i, j: (0, j)),
            pl.BlockSpec((tf, d), lambda i, j: (j, 0)),
        ],
        out_shape=[out_sds, jax.ShapeDtypeStruct(w_gate.shape, BF16),
                   jax.ShapeDtypeStruct(w_up.shape, BF16), jax.ShapeDtypeStruct(w_down.shape, BF16)],
        scratch_shapes=[pltpu.VMEM((tm, d), BF16)],
        compiler_params=_params("arbitrary", "arbitrary"),
        name="ffn_head",
    )(h, g_pre, w_gate, w_up, w_down, g_post)

    tf = FFN_TF
    return pl.pallas_call(
        _ffn_main_kernel,
        grid=(t // tm - 1, d_ff // tf),
        in_specs=[
            pl.BlockSpec((tm, d), lambda i, j: (i + 1, 0)),
            _resident((1, d)),
            pl.BlockSpec((d, tf), lambda i, j: (0, j)),
            pl.BlockSpec((d, tf), lambda i, j: (0, j)),
            pl.BlockSpec((tf, d), lambda i, j: (j, 0)),
            _resident((1, d)),
            pl.BlockSpec(memory_space=pl.ANY),
        ],
        out_specs=pl.BlockSpec((tm, d), lambda i, j: (i + 1, 0)),
        out_shape=out_sds,
        input_output_aliases={6: 0},
        scratch_shapes=[pltpu.VMEM((tm, d), BF16)],
        compiler_params=_params("parallel", "arbitrary"),
        name="ffn",
    )(h, g_pre, wg, wu, wd, g_post, head_out)


def _qkv_kernel(h_ref, g_ref, w_ref, b_ref, nat_ref, a4_ref, a16_ref, zb_ref, zs_ref, z4_ref):
    tm = h_ref.shape[0]
    nc = zs_ref.shape[0]
    per_chunk = MXU_WIDTH // LANES

    def copy(c):
        for r in range(4):
            z4 = zs_ref[c, pl.ds(r, tm // 4, stride=4), :]
            a4_ref[0, c, r] = z4.astype(BF16)
            z4_ref[c % 2, r] = z4
        for r in range(16):
            a16_ref[0, c, r] = z4_ref[c % 2, r % 4, pl.ds(r // 4, tm // 16, stride=4), :].astype(BF16)

    lag = 2 * per_chunk
    xn = _rms(h_ref[...], g_ref[...]).astype(BF16)
    zb_ref[...] = (_dot(xn, w_ref[:, nc * LANES:]) + b_ref[:, nc * LANES:]).astype(BF16)
    for c2 in range(nc // per_chunk):
        wide = slice(c2 * MXU_WIDTH, (c2 + 1) * MXU_WIDTH)
        z2 = _dot(xn, w_ref[:, wide]) + b_ref[:, wide]
        for half in range(per_chunk):
            c = c2 * per_chunk + half
            zc = z2[:, half * LANES:(half + 1) * LANES]
            zs_ref[c] = zc
            nat_ref[c] = zc.astype(BF16)
            if c >= lag:
                copy(c - lag)
    for c in range(nc - lag, nc):
        copy(c)


def _qkv(h, g, w, b, bsz, seq):
    t, d = h.shape
    tm = QKV_TM
    nc = 3 * A_WIDTH // LANES
    per_seq = seq // tm
    return pl.pallas_call(
        _qkv_kernel,
        grid=(t // tm,),
        in_specs=[
            pl.BlockSpec((tm, d), lambda i: (i, 0)),
            _resident((1, d)),
            _resident(w.shape),
            _resident(b.shape),
        ],
        out_specs=[
            pl.BlockSpec((nc, tm, LANES), lambda i: (0, i, 0)),
            pl.BlockSpec((1, nc, 4, tm // 4, LANES), lambda i: (i // per_seq, 0, 0, i % per_seq, 0)),
            pl.BlockSpec((1, nc, 16, tm // 16, LANES), lambda i: (i // per_seq, 0, 0, i % per_seq, 0)),
            pl.BlockSpec((tm, ZB_WIDTH), lambda i: (i, 0)),
        ],
        out_shape=[
            jax.ShapeDtypeStruct((nc, t, LANES), BF16),
            jax.ShapeDtypeStruct((bsz, nc, 4, seq // 4, LANES), BF16),
            jax.ShapeDtypeStruct((bsz, nc, 16, seq // 16, LANES), BF16),
            jax.ShapeDtypeStruct((t, ZB_WIDTH), BF16),
        ],
        scratch_shapes=[pltpu.VMEM((nc, tm, LANES), F32), pltpu.VMEM((2, 4, tm // 4, LANES), F32)],
        compiler_params=_params("parallel"),
        name="qkv",
    )(h, g, w, b)


def _attn_a_kernel(slopes_ref,
                   q1_ref, k1_ref, v1_ref, q4_ref, k4_ref, v4_ref, q16_ref, k16_ref, v16_ref,
                   o_ref, acc_ref, m_ref, l_ref, bias_ref, bias16_ref):
    hp = pl.program_id(1)
    seq = q1_ref.shape[1]
    lane = lax.broadcasted_iota(jnp.int32, (1, LANES), 1)
    head0 = lane < HEAD_DIM
    neg_slopes = (-slopes_ref[4 * hp], -slopes_ref[4 * hp + 2])

    def bias_table(nk, lead, dil):
        qi = lax.broadcasted_iota(jnp.int32, (BLOCK, nk), 0)
        kj = lax.broadcasted_iota(jnp.int32, (BLOCK, nk), 1)
        rel = qi - kj if lead else qi + (nk - BLOCK) - kj
        valid = (rel >= 0) & (rel <= A_WINDOW)
        rel_f = rel.astype(F32)
        return jnp.concatenate(
            [jnp.where(valid, rel_f * (ns * float(dil)), NEG_INF) for ns in neg_slopes], axis=0)

    for bi, dil in enumerate(DILATIONS[:2]):
        bias_ref[bi, 0] = bias_table(2 * BLOCK, False, dil)
        bias_ref[bi, 1] = bias_table(2 * BLOCK, True, dil)
    bias16_ref[...] = bias_table(BLOCK, True, DILATIONS[2])

    def unit(q, k, v, bias):
        zero = jnp.zeros_like(q)
        qs = jnp.concatenate([jnp.where(head0, q, zero), jnp.where(head0, zero, q)], axis=0)
        s = _dot_nt(qs * QK_SCALE, k) + bias
        m = jnp.max(s, axis=-1, keepdims=True)
        e = jnp.exp(s - m)
        l = jnp.sum(e, axis=-1, keepdims=True)
        pv = _dot(e.astype(BF16), v)
        return (jnp.where(head0, pv[:BLOCK], pv[BLOCK:]),
                jnp.where(head0, m[:BLOCK], m[BLOCK:]),
                jnp.where(head0, l[:BLOCK], l[BLOCK:]))

    def store(branch, rows, res):
        acc_ref[branch, rows, :] = res[0]
        m_ref[branch, rows, :] = res[1]
        l_ref[branch, rows, :] = res[2]

    unroll = ATTN_A_UNITS_PER_ITER
    nb4 = seq // 4 // BLOCK

    def body1(g, carry):
        for i in range(unroll):
            blk = g * unroll + i
            q0 = pl.multiple_of(blk * BLOCK, BLOCK)
            if i == 0:
                k0 = pl.multiple_of(jnp.maximum(blk - 1, 0) * BLOCK, BLOCK)
                bias = bias_ref[0, jnp.where(blk == 0, 1, 0)]
            else:
                k0 = pl.multiple_of((blk - 1) * BLOCK, BLOCK)
                bias = bias_ref[0, 0]
            store(0, pl.ds(q0, BLOCK),
                  unit(q1_ref[0, pl.ds(q0, BLOCK), :], k1_ref[0, pl.ds(k0, 2 * BLOCK), :],
                       v1_ref[0, pl.ds(k0, 2 * BLOCK), :], bias))
        return carry

    lax.fori_loop(0, seq // BLOCK // unroll, body1, 0)

    def body4(g, carry):
        for i in range(unroll):
            r = g * (unroll // nb4) + i // nb4
            blk = i % nb4
            k0 = max(blk - 1, 0) * BLOCK
            store(1, pl.ds(r + 4 * blk * BLOCK, BLOCK, stride=4),
                  unit(q4_ref[0, r, pl.ds(blk * BLOCK, BLOCK), :],
                       k4_ref[0, r, pl.ds(k0, 2 * BLOCK), :],
                       v4_ref[0, r, pl.ds(k0, 2 * BLOCK), :], bias_ref[1, 1 if blk == 0 else 0]))
        return carry

    lax.fori_loop(0, 4 * nb4 // unroll, body4, 0)

    def body16(g, carry):
        for i in range(unroll):
            r = g * unroll + i
            store(2, pl.ds(r, BLOCK, stride=16),
                  unit(q16_ref[0, r], k16_ref[0, r], v16_ref[0, r], bias16_ref[...]))
        return carry

    lax.fori_loop(0, 16 // unroll, body16, 0)

    def merge(blk, carry):
        rows = pl.ds(pl.multiple_of(blk * (2 * BLOCK), 2 * BLOCK), 2 * BLOCK)
        m0, m1, m2 = m_ref[0, rows, :], m_ref[1, rows, :], m_ref[2, rows, :]
        mx = jnp.maximum(jnp.maximum(m0, m1), m2)
        w0, w1, w2 = jnp.exp(m0 - mx), jnp.exp(m1 - mx), jnp.exp(m2 - mx)
        num = w0 * acc_ref[0, rows, :] + w1 * acc_ref[1, rows, :] + w2 * acc_ref[2, rows, :]
        den = w0 * l_ref[0, rows, :] + w1 * l_ref[1, rows, :] + w2 * l_ref[2, rows, :]
        o_ref[0, rows, :] = num / den
        return carry

    lax.fori_loop(0, seq // (2 * BLOCK), merge, 0)


def _attn_a(slopes, nat, a4, a16):
    _, bsz, seq, _ = nat.shape
    pairs = A_WIDTH // LANES

    def col(base):
        return lambda b, hp: (base + hp, b, 0, 0)

    def col4(base):
        return lambda b, hp: (b, base + hp, 0, 0, 0)

    nat_spec = [pl.BlockSpec((None, 1, seq, LANES), col(i * pairs)) for i in range(3)]
    a4_spec = [pl.BlockSpec((1, None, 4, seq // 4, LANES), col4(i * pairs)) for i in range(3)]
    a16_spec = [pl.BlockSpec((1, None, 16, seq // 16, LANES), col4(i * pairs)) for i in range(3)]
    return pl.pallas_call(
        _attn_a_kernel,
        grid=(bsz, pairs),
        in_specs=[pl.BlockSpec(memory_space=pltpu.SMEM)] + nat_spec + a4_spec + a16_spec,
        out_specs=pl.BlockSpec((1, seq, LANES), lambda b, hp: (b, 0, hp)),
        out_shape=jax.ShapeDtypeStruct((bsz, seq, A_WIDTH), F32),
        scratch_shapes=[pltpu.VMEM((3, seq, LANES), F32)] * 3 + [
            pltpu.VMEM((2, 2, 2 * BLOCK, 2 * BLOCK), F32), pltpu.VMEM((2 * BLOCK, BLOCK), F32)],
        compiler_params=_params("parallel", "parallel"),
        name="attn_a",
    )(slopes, nat, nat, nat, a4, a4, a4, a16, a16, a16)


def _attn_b_kernel(slopes_ref, sinks_ref, q_ref, k_ref, v_ref, o_ref, kd_ref, vd_ref, bias_ref):
    g = pl.program_id(1)
    seq = q_ref.shape[1]
    lane = lax.broadcasted_iota(jnp.int32, (1, LANES), 1)
    head0 = lane < HEAD_DIM
    mine = (lane >= HEAD_DIM).astype(jnp.int32) == g

    def both_halves(ref):
        x = ref[0].astype(F32)
        return jnp.where(mine, x, pltpu.roll(x, HEAD_DIM, 1)).astype(BF16)

    kd_ref[...] = both_halves(k_ref)
    vd_ref[...] = both_halves(v_ref)

    neg_slopes = [-slopes_ref[2 * (g * B_GROUP + j) + 1] for j in range(B_GROUP)]
    sinks = [sinks_ref[g * B_GROUP + j] for j in range(B_GROUP)]

    for lead in (False, True):
        qi = lax.broadcasted_iota(jnp.int32, (BLOCK, 2 * BLOCK), 0)
        kj = lax.broadcasted_iota(jnp.int32, (BLOCK, 2 * BLOCK), 1)
        rel = qi - kj if lead else qi + BLOCK - kj
        valid = (rel >= 0) & (rel <= B_WINDOW)
        rel_f = rel.astype(F32)
        for j in range(B_GROUP):
            bias_ref[int(lead), pl.ds(j * BLOCK, BLOCK), :] = jnp.where(valid, rel_f * neg_slopes[j], NEG_INF)

    def unit(blk, lead_possible):
        q0 = pl.multiple_of(blk * BLOCK, BLOCK)
        if lead_possible:
            k0 = pl.multiple_of(jnp.maximum(blk - 1, 0) * BLOCK, BLOCK)
            bias = bias_ref[jnp.where(blk == 0, 1, 0)]
        else:
            k0 = pl.multiple_of((blk - 1) * BLOCK, BLOCK)
            bias = bias_ref[0]
        k = kd_ref[pl.ds(k0, 2 * BLOCK), :]
        v = vd_ref[pl.ds(k0, 2 * BLOCK), :]
        q = q_ref[0, pl.ds(q0, BLOCK), :] * QK_SCALE
        stacked = []
        for j in range(B_GROUP):
            qp = q[:, (j // 2) * LANES:(j // 2 + 1) * LANES]
            stacked.append(jnp.where(head0 if j % 2 == 0 else ~head0, qp, jnp.zeros_like(qp)))
        s_all = _dot_nt(jnp.concatenate(stacked, axis=0), k) + bias
        probs, inv = [], []
        for j in range(B_GROUP):
            s = s_all[j * BLOCK:(j + 1) * BLOCK]
            m = jnp.maximum(jnp.max(s, axis=-1, keepdims=True), sinks[j])
            e = jnp.exp(s - m)
            inv.append(1.0 / (jnp.sum(e, axis=-1, keepdims=True) + jnp.exp(sinks[j] - m)))
            probs.append(e.astype(BF16))
        pv = _dot(jnp.concatenate(probs, axis=0), v)
        for p in range(B_GROUP // 2):
            even = pv[(2 * p) * BLOCK:(2 * p + 1) * BLOCK]
            odd = pv[(2 * p + 1) * BLOCK:(2 * p + 2) * BLOCK]
            o_ref[0, pl.ds(q0, BLOCK), p * LANES:(p + 1) * LANES] = (
                jnp.where(head0, even, odd) * jnp.where(head0, inv[2 * p], inv[2 * p + 1]))

    def body(it, carry):
        for i in range(ATTN_B_UNITS_PER_ITER):
            unit(it * ATTN_B_UNITS_PER_ITER + i, i == 0)
        return carry

    lax.fori_loop(0, seq // BLOCK // ATTN_B_UNITS_PER_ITER, body, 0)


def _attn_b(slopes, sinks, zb):
    bsz, seq, _ = zb.shape
    qw = B_GROUP * HEAD_DIM
    k_blk = B_WIDTH // LANES
    return pl.pallas_call(
        _attn_b_kernel,
        grid=(bsz, B_KV_HEADS),
        in_specs=[
            pl.BlockSpec(memory_space=pltpu.SMEM),
            pl.BlockSpec(memory_space=pltpu.SMEM),
            pl.BlockSpec((1, seq, qw), lambda b, g: (b, 0, g)),
            pl.BlockSpec((1, seq, LANES), lambda b, g: (b, 0, k_blk)),
            pl.BlockSpec((1, seq, LANES), lambda b, g: (b, 0, k_blk + 1)),
        ],
        out_specs=pl.BlockSpec((1, seq, qw), lambda b, g: (b, 0, g)),
        out_shape=jax.ShapeDtypeStruct((bsz, seq, B_WIDTH), F32),
        scratch_shapes=[pltpu.VMEM((seq, LANES), BF16)] * 2 + [
            pltpu.VMEM((2, B_GROUP * BLOCK, 2 * BLOCK), F32)],
        compiler_params=_params("parallel", "parallel"),
        name="attn_b",
    )(slopes, sinks, zb, zb, zb)


def _out_kernel(h_ref, oa_ref, ob_ref, ga_ref, gb_ref, wo_ref, bo_ref, gpost_ref, o_ref):
    na = _rms(oa_ref[...], ga_ref[...]).astype(BF16)
    nb = _rms(ob_ref[...], gb_ref[...]).astype(BF16)
    o = _dot(na, wo_ref[0:A_WIDTH, :]) + _dot(nb, wo_ref[A_WIDTH:A_WIDTH + B_WIDTH, :]) + bo_ref[...]
    o_ref[...] = h_ref[...] + _rms(o, gpost_ref[...])


def _out_proj(h, oa, ob, g_a, g_b, w_o, b_o, g_post):
    t, d = h.shape
    tm = OUT_TM
    row = lambda i: (i, 0)
    return pl.pallas_call(
        _out_kernel,
        grid=(t // tm,),
        in_specs=[
            pl.BlockSpec((tm, d), row),
            pl.BlockSpec((tm, A_WIDTH), row),
            pl.BlockSpec((tm, B_WIDTH), row),
            _resident((1, A_WIDTH)),
            _resident((1, B_WIDTH)),
            _resident((A_WIDTH + B_WIDTH, d)),
            _resident((1, d)),
            _resident((1, d)),
        ],
        out_specs=pl.BlockSpec((tm, d), row),
        out_shape=jax.ShapeDtypeStruct((t, d), F32),
        compiler_params=_params("parallel"),
        name="out_proj",
    )(h, oa, ob, g_a, g_b, w_o, b_o, g_post)


def _ple_kernel(h_ref, p_ref, gpre_ref, wg_ref, wp_ref, gpost_ref, o_ref):
    h = h_ref[...]
    gate = jax.nn.sigmoid(_dot(_rms(h, gpre_ref[...]).astype(BF16), wg_ref[...]))
    e = _dot(p_ref[...].astype(BF16), wp_ref[...])
    o_ref[...] = h + _rms(gate * e, gpost_ref[...])


def _ple(h, p, g_pre, w_gate, w_proj, g_post):
    t, d = h.shape
    pd = p.shape[1]
    tm = PLE_TM
    row = lambda i: (i, 0)
    return pl.pallas_call(
        _ple_kernel,
        grid=(t // tm,),
        in_specs=[
            pl.BlockSpec((tm, d), row),
            pl.BlockSpec((tm, pd), row),
            _resident((1, d)),
            _resident((d, d)),
            _resident((pd, d)),
            _resident((1, d)),
        ],
        out_specs=pl.BlockSpec((tm, d), row),
        out_shape=jax.ShapeDtypeStruct((t, d), F32),
        compiler_params=_params("parallel"),
        name="ple",
    )(h, p, g_pre, w_gate, w_proj, g_post)


def _alibi_slopes(n):
    return jnp.exp2(-ALIBI_MAX_BIAS * (jnp.arange(n, dtype=F32) + 1.0) / n)


def kernel(x, p, g_ffn1_pre, w_ffn1_gate, w_ffn1_up, w_ffn1_down, g_ffn1_post, g_mix_pre, w_qkv, b_qkv, attn_sinks, g_out_a, g_out_b, w_o, b_o, g_mix_post, g_ffn2_pre, w_ffn2_gate, w_ffn2_up, w_ffn2_down, g_ffn2_post, g_ple_pre, w_ple_gate, w_ple_proj, g_ple_post):
    bsz, seq, d = x.shape
    depth = p.shape[0]
    t = bsz * seq
    slopes = _alibi_slopes(A_HEADS + B_Q_HEADS)
    bf = lambda w: w.astype(BF16)
    h = x.reshape(t, d)
    for i in range(depth):
        h = _ffn(h, g_ffn1_pre[i][None], w_ffn1_gate[i], w_ffn1_up[i], w_ffn1_down[i], g_ffn1_post[i][None])
        nat, a4, a16, zb = _qkv(h, g_mix_pre[i][None], bf(w_qkv[i]), b_qkv[i][None], bsz, seq)
        oa = _attn_a(slopes, nat.reshape(-1, bsz, seq, LANES), a4, a16)
        ob = _attn_b(slopes, attn_sinks[i].reshape(-1), zb.reshape(bsz, seq, ZB_WIDTH))
        h = _out_proj(h, oa.reshape(t, A_WIDTH), ob.reshape(t, B_WIDTH), g_out_a[i][None], g_out_b[i][None],
                      bf(w_o[i]), b_o[i][None], g_mix_post[i][None])
        h = _ffn(h, g_ffn2_pre[i][None], w_ffn2_gate[i], w_ffn2_up[i], w_ffn2_down[i], g_ffn2_post[i][None])
        h = _ple(h, p[i].reshape(t, -1), g_ple_pre[i][None], bf(w_ple_gate[i]), bf(w_ple_proj[i]),
                 g_ple_post[i][None])
    return h.reshape(bsz, seq, d)
```

```python
import jax
import jax.numpy as jnp
from jax import lax
from jax.experimental import pallas as pl
from jax.experimental.pallas import tpu as pltpu

F32 = jnp.float32
BF16 = jnp.bfloat16

HEAD_DIM = 64
LANES = 128
MXU_WIDTH = 256
BLOCK = 128
A_HEADS = 16
B_Q_HEADS = 16
B_KV_HEADS = 2
B_GROUP = B_Q_HEADS // B_KV_HEADS
A_WIDTH = A_HEADS * HEAD_DIM
B_WIDTH = B_Q_HEADS * HEAD_DIM
KV_B_WIDTH = B_KV_HEADS * HEAD_DIM
ZB_WIDTH = B_WIDTH + 2 * KV_B_WIDTH
DILATIONS = (1, 4, 16)
A_WINDOW = 128
B_WINDOW = 127
ALIBI_MAX_BIAS = 8.0
EPS = 1e-6
QK_SCALE = HEAD_DIM ** -0.5
NEG_INF = float("-inf")

VMEM_LIMIT = 61 * 1024 * 1024

NORM_ROWS = 256
FFN_TM = 1024
FFN_TF = 512
FFN_SPLIT = 2
FFN_HEAD_TF = 256
QKV_TM = 512
ATTN_A_UNITS_PER_ITER = 16
ATTN_B_UNITS_PER_ITER = 8
OUT_TM = 512
PLE_TM = 1024


def _rms(x, g):
    return x * lax.rsqrt(jnp.mean(x * x, axis=-1, keepdims=True) + EPS) * g


def _dot(a, b):
    return jnp.dot(a, b, preferred_element_type=F32)


def _dot_nt(a, b):
    return lax.dot_general(a, b, (((1,), (1,)), ((), ())), preferred_element_type=F32)


def _resident(shape):
    return pl.BlockSpec(shape, lambda *_: (0,) * len(shape), pipeline_mode=pl.Buffered(1))


def _params(*semantics):
    return pltpu.CompilerParams(dimension_semantics=semantics, vmem_limit_bytes=VMEM_LIMIT)


def _ffn_steps(x_ref, gpre_ref, weights, gpost_ref, o_ref, xn_ref):
    j = pl.program_id(1)
    last = pl.num_programs(1) - 1
    tm = x_ref.shape[0]
    pieces = [pl.ds(i * (tm // FFN_SPLIT), tm // FFN_SPLIT) for i in range(FFN_SPLIT)]

    def prenorm(rows):
        for c in range(rows.size // NORM_ROWS):
            r = pl.ds(rows.start + c * NORM_ROWS, NORM_ROWS)
            xn_ref[r, :] = _rms(x_ref[r, :], gpre_ref[...]).astype(BF16)

    def postnorm(rows):
        for c in range(rows.size // NORM_ROWS):
            r = pl.ds(rows.start + c * NORM_ROWS, NORM_ROWS)
            o_ref[r, :] = x_ref[r, :] + 0.5 * _rms(o_ref[r, :], gpost_ref[...])

    def swiglu(w, rows, accumulate):
        wg, wu, wd = w
        xn = xn_ref[rows, :]
        gate = _dot(xn, wg)
        up = _dot(xn, wu)
        down = _dot((gate * jax.nn.sigmoid(gate) * up).astype(BF16), wd)
        o_ref[rows, :] = o_ref[rows, :] + down if accumulate else down

    @pl.when(j == 0)
    def _():
        w = weights()
        for rows in pieces:
            prenorm(rows)
        for rows in pieces:
            swiglu(w, rows, False)

    @pl.when((j > 0) & (j < last))
    def _():
        swiglu(weights(), pl.ds(0, tm), True)

    @pl.when(j == last)
    def _():
        w = weights()
        for rows in pieces:
            swiglu(w, rows, True)
        for rows in pieces:
            postnorm(rows)


def _ffn_head_kernel(x_ref, gpre_ref, wg32_ref, wu32_ref, wd32_ref, gpost_ref,
                     o_ref, wg_ref, wu_ref, wd_ref, xn_ref):
    def weights():
        w = [r[...].astype(BF16) for r in (wg32_ref, wu32_ref, wd32_ref)]
        for dst, val in zip((wg_ref, wu_ref, wd_ref), w):
            dst[...] = val
        return w

    _ffn_steps(x_ref, gpre_ref, weights, gpost_ref, o_ref, xn_ref)


def _ffn_main_kernel(x_ref, gpre_ref, wg_ref, wu_ref, wd_ref, gpost_ref, head_out_ref, o_ref, xn_ref):
    del head_out_ref
    _ffn_steps(x_ref, gpre_ref, lambda: (wg_ref[...], wu_ref[...], wd_ref[...]), gpost_ref, o_ref, xn_ref)


def _ffn(h, g_pre, w_gate, w_up, w_down, g_post):
    t, d = h.shape
    d_ff = w_gate.shape[1]
    out_sds = jax.ShapeDtypeStruct((t, d), F32)

    tm, tf = FFN_TM, FFN_HEAD_TF
    head_out, wg, wu, wd = pl.pallas_call(
        _ffn_head_kernel,
        grid=(1, d_ff // tf),
        in_specs=[
            pl.BlockSpec((tm, d), lambda i, j: (0, 0), pipeline_mode=pl.Buffered(1)),
            _resident((1, d)),
            pl.BlockSpec((d, tf), lambda i, j: (0, j)),
            pl.BlockSpec((d, tf), lambda i, j: (0, j)),
            pl.BlockSpec((tf, d), lambda i, j: (j, 0)),
            _resident((1, d)),
        ],
        out_specs=[
            pl.BlockSpec((tm, d), lambda i, j: (0, 0)),
            pl.BlockSpec((d, tf), lambda i, j: (0, j)),
            pl.BlockSpec((d, tf), lambda i, j: (0, j)),
            pl.BlockSpec((tf, d), lambda i, j: (j, 0)),
        ],
        out_shape=[out_sds, jax.ShapeDtypeStruct(w_gate.shape, BF16),
                   jax.ShapeDtypeStruct(w_up.shape, BF16), jax.ShapeDtypeStruct(w_down.shape, BF16)],
        scratch_shapes=[pltpu.VMEM((tm, d), BF16)],
        compiler_params=_params("arbitrary", "arbitrary"),
        name="ffn_head",
    )(h, g_pre, w_gate, w_up, w_down, g_post)

    tf = FFN_TF
    return pl.pallas_call(
        _ffn_main_kernel,
        grid=(t // tm - 1, d_ff // tf),
        in_specs=[
            pl.BlockSpec((tm, d), lambda i, j: (i + 1, 0)),
            _resident((1, d)),
            pl.BlockSpec((d, tf), lambda i, j: (0, j)),
            pl.BlockSpec((d, tf), lambda i, j: (0, j)),
            pl.BlockSpec((tf, d), lambda i, j: (j, 0)),
            _resident((1, d)),
            pl.BlockSpec(memory_space=pl.ANY),
        ],
        out_specs=pl.BlockSpec((tm, d), lambda i, j: (i + 1, 0)),
        out_shape=out_sds,
        input_output_aliases={6: 0},
        scratch_shapes=[pltpu.VMEM((tm, d), BF16)],
        compiler_params=_params("parallel", "arbitrary"),
        name="ffn",
    )(h, g_pre, wg, wu, wd, g_post, head_out)


def _qkv_kernel(h_ref, g_ref, w_ref, b_ref, nat_ref, a4_ref, a16_ref, zb_ref, zs_ref, z4_ref):
    tm = h_ref.shape[0]
    nc = zs_ref.shape[0]
    per_chunk = MXU_WIDTH // LANES
    xn = _rms(h_ref[...], g_ref[...]).astype(BF16)
    zb_ref[...] = (_dot(xn, w_ref[:, nc * LANES:]) + b_ref[:, nc * LANES:]).astype(BF16)
    for c2 in range(nc // per_chunk):
        wide = slice(c2 * MXU_WIDTH, (c2 + 1) * MXU_WIDTH)
        z2 = _dot(xn, w_ref[:, wide]) + b_ref[:, wide]
        for half in range(per_chunk):
            c = c2 * per_chunk + half
            zc = z2[:, half * LANES:(half + 1) * LANES]
            zs_ref[c] = zc
            nat_ref[c] = zc.astype(BF16)

    def copy(c, carry):
        for r in range(4):
            z4 = zs_ref[c, pl.ds(r, tm // 4, stride=4), :]
            a4_ref[0, c, r] = z4.astype(BF16)
            z4_ref[r] = z4
        for r in range(16):
            a16_ref[0, c, r] = z4_ref[r % 4, pl.ds(r // 4, tm // 16, stride=4), :].astype(BF16)
        return carry

    lax.fori_loop(0, nc, copy, 0)


def _qkv(h, g, w, b, bsz, seq):
    t, d = h.shape
    tm = QKV_TM
    nc = 3 * A_WIDTH // LANES
    per_seq = seq // tm
    return pl.pallas_call(
        _qkv_kernel,
        grid=(t // tm,),
        in_specs=[
            pl.BlockSpec((tm, d), lambda i: (i, 0)),
            _resident((1, d)),
            _resident(w.shape),
            _resident(b.shape),
        ],
        out_specs=[
            pl.BlockSpec((nc, tm, LANES), lambda i: (0, i, 0)),
            pl.BlockSpec((1, nc, 4, tm // 4, LANES), lambda i: (i // per_seq, 0, 0, i % per_seq, 0)),
            pl.BlockSpec((1, nc, 16, tm // 16, LANES), lambda i: (i // per_seq, 0, 0, i % per_seq, 0)),
            pl.BlockSpec((tm, ZB_WIDTH), lambda i: (i, 0)),
        ],
        out_shape=[
            jax.ShapeDtypeStruct((nc, t, LANES), BF16),
            jax.ShapeDtypeStruct((bsz, nc, 4, seq // 4, LANES), BF16),
            jax.ShapeDtypeStruct((bsz, nc, 16, seq // 16, LANES), BF16),
            jax.ShapeDtypeStruct((t, ZB_WIDTH), BF16),
        ],
        scratch_shapes=[pltpu.VMEM((nc, tm, LANES), F32), pltpu.VMEM((4, tm // 4, LANES), F32)],
        compiler_params=_params("parallel"),
        name="qkv",
    )(h, g, w, b)


def _attn_a_kernel(slopes_ref,
                   q1_ref, k1_ref, v1_ref, q4_ref, k4_ref, v4_ref, q16_ref, k16_ref, v16_ref,
                   o_ref, acc_ref, m_ref, l_ref, bias_ref, bias16_ref):
    hp = pl.program_id(1)
    seq = q1_ref.shape[1]
    lane = lax.broadcasted_iota(jnp.int32, (1, LANES), 1)
    head0 = lane < HEAD_DIM
    neg_slopes = (-slopes_ref[4 * hp], -slopes_ref[4 * hp + 2])

    def bias_table(nk, lead, dil):
        qi = lax.broadcasted_iota(jnp.int32, (BLOCK, nk), 0)
        kj = lax.broadcasted_iota(jnp.int32, (BLOCK, nk), 1)
        rel = qi - kj if lead else qi + (nk - BLOCK) - kj
        valid = (rel >= 0) & (rel <= A_WINDOW)
        rel_f = rel.astype(F32)
        return jnp.concatenate(
            [jnp.where(valid, rel_f * (ns * float(dil)), NEG_INF) for ns in neg_slopes], axis=0)

    for bi, dil in enumerate(DILATIONS[:2]):
        bias_ref[bi, 0] = bias_table(2 * BLOCK, False, dil)
        bias_ref[bi, 1] = bias_table(2 * BLOCK, True, dil)
    bias16_ref[...] = bias_table(BLOCK, True, DILATIONS[2])

    def unit(q, k, v, bias):
        zero = jnp.zeros_like(q)
        qs = jnp.concatenate([jnp.where(head0, q, zero), jnp.where(head0, zero, q)], axis=0)
        s = _dot_nt(qs * QK_SCALE, k) + bias
        m = jnp.max(s, axis=-1, keepdims=True)
        e = jnp.exp(s - m)
        l = jnp.sum(e, axis=-1, keepdims=True)
        pv = _dot(e.astype(BF16), v)
        return (jnp.where(head0, pv[:BLOCK], pv[BLOCK:]),
                jnp.where(head0, m[:BLOCK], m[BLOCK:]),
                jnp.where(head0, l[:BLOCK], l[BLOCK:]))

    def store(branch, rows, res):
        acc_ref[branch, rows, :] = res[0]
        m_ref[branch, rows, :] = res[1]
        l_ref[branch, rows, :] = res[2]

    unroll = ATTN_A_UNITS_PER_ITER
    nb4 = seq // 4 // BLOCK

    def body1(g, carry):
        for i in range(unroll):
            blk = g * unroll + i
            q0 = pl.multiple_of(blk * BLOCK, BLOCK)
            if i == 0:
                k0 = pl.multiple_of(jnp.maximum(blk - 1, 0) * BLOCK, BLOCK)
                bias = bias_ref[0, jnp.where(blk == 0, 1, 0)]
            else:
                k0 = pl.multiple_of((blk - 1) * BLOCK, BLOCK)
                bias = bias_ref[0, 0]
            store(0, pl.ds(q0, BLOCK),
                  unit(q1_ref[0, pl.ds(q0, BLOCK), :], k1_ref[0, pl.ds(k0, 2 * BLOCK), :],
                       v1_ref[0, pl.ds(k0, 2 * BLOCK), :], bias))
        return carry

    lax.fori_loop(0, seq // BLOCK // unroll, body1, 0)

    def body4(g, carry):
        for i in range(unroll):
            r = g * (unroll // nb4) + i // nb4
            blk = i % nb4
            k0 = max(blk - 1, 0) * BLOCK
            store(1, pl.ds(r + 4 * blk * BLOCK, BLOCK, stride=4),
                  unit(q4_ref[0, r, pl.ds(blk * BLOCK, BLOCK), :],
                       k4_ref[0, r, pl.ds(k0, 2 * BLOCK), :],
                       v4_ref[0, r, pl.ds(k0, 2 * BLOCK), :], bias_ref[1, 1 if blk == 0 else 0]))
        return carry

    lax.fori_loop(0, 4 * nb4 // unroll, body4, 0)

    def body16(g, carry):
        for i in range(unroll):
            r = g * unroll + i
            store(2, pl.ds(r, BLOCK, stride=16),
                  unit(q16_ref[0, r], k16_ref[0, r], v16_ref[0, r], bias16_ref[...]))
        return carry

    lax.fori_loop(0, 16 // unroll, body16, 0)

    def merge(blk, carry):
        rows = pl.ds(pl.multiple_of(blk * (2 * BLOCK), 2 * BLOCK), 2 * BLOCK)
        m0, m1, m2 = m_ref[0, rows, :], m_ref[1, rows, :], m_ref[2, rows, :]
        mx = jnp.maximum(jnp.maximum(m0, m1), m2)
        w0, w1, w2 = jnp.exp(m0 - mx), jnp.exp(m1 - mx), jnp.exp(m2 - mx)
        num = w0 * acc_ref[0, rows, :] + w1 * acc_ref[1, rows, :] + w2 * acc_ref[2, rows, :]
        den = w0 * l_ref[0, rows, :] + w1 * l_ref[1, rows, :] + w2 * l_ref[2, rows, :]
        o_ref[0, rows, :] = num / den
        return carry

    lax.fori_loop(0, seq // (2 * BLOCK), merge, 0)


def _attn_a(slopes, nat, a4, a16):
    _, bsz, seq, _ = nat.shape
    pairs = A_WIDTH // LANES

    def col(base):
        return lambda b, hp: (base + hp, b, 0, 0)

    def col4(base):
        return lambda b, hp: (b, base + hp, 0, 0, 0)

    nat_spec = [pl.BlockSpec((None, 1, seq, LANES), col(i * pairs)) for i in range(3)]
    a4_spec = [pl.BlockSpec((1, None, 4, seq // 4, LANES), col4(i * pairs)) for i in range(3)]
    a16_spec = [pl.BlockSpec((1, None, 16, seq // 16, LANES), col4(i * pairs)) for i in range(3)]
    return pl.pallas_call(
        _attn_a_kernel,
        grid=(bsz, pairs),
        in_specs=[pl.BlockSpec(memory_space=pltpu.SMEM)] + nat_spec + a4_spec + a16_spec,
        out_specs=pl.BlockSpec((1, seq, LANES), lambda b, hp: (b, 0, hp)),
        out_shape=jax.ShapeDtypeStruct((bsz, seq, A_WIDTH), F32),
        scratch_shapes=[pltpu.VMEM((3, seq, LANES), F32)] * 3 + [
            pltpu.VMEM((2, 2, 2 * BLOCK, 2 * BLOCK), F32), pltpu.VMEM((2 * BLOCK, BLOCK), F32)],
        compiler_params=_params("parallel", "parallel"),
        name="attn_a",
    )(slopes, nat, nat, nat, a4, a4, a4, a16, a16, a16)


def _attn_b_kernel(slopes_ref, sinks_ref, q_ref, k_ref, v_ref, o_ref, kd_ref, vd_ref, bias_ref):
    g = pl.program_id(1)
    seq = q_ref.shape[1]
    lane = lax.broadcasted_iota(jnp.int32, (1, LANES), 1)
    head0 = lane < HEAD_DIM
    mine = (lane >= HEAD_DIM).astype(jnp.int32) == g

    def both_halves(ref):
        x = ref[0].astype(F32)
        return jnp.where(mine, x, pltpu.roll(x, HEAD_DIM, 1)).astype(BF16)

    kd_ref[...] = both_halves(k_ref)
    vd_ref[...] = both_halves(v_ref)

    neg_slopes = [-slopes_ref[2 * (g * B_GROUP + j) + 1] for j in range(B_GROUP)]
    sinks = [sinks_ref[g * B_GROUP + j] for j in range(B_GROUP)]

    for lead in (False, True):
        qi = lax.broadcasted_iota(jnp.int32, (BLOCK, 2 * BLOCK), 0)
        kj = lax.broadcasted_iota(jnp.int32, (BLOCK, 2 * BLOCK), 1)
        rel = qi - kj if lead else qi + BLOCK - kj
        valid = (rel >= 0) & (rel <= B_WINDOW)
        rel_f = rel.astype(F32)
        for j in range(B_GROUP):
            bias_ref[int(lead), pl.ds(j * BLOCK, BLOCK), :] = jnp.where(valid, rel_f * neg_slopes[j], NEG_INF)

    def unit(blk, lead_possible):
        q0 = pl.multiple_of(blk * BLOCK, BLOCK)
        if lead_possible:
            k0 = pl.multiple_of(jnp.maximum(blk - 1, 0) * BLOCK, BLOCK)
            bias = bias_ref[jnp.where(blk == 0, 1, 0)]
        else:
            k0 = pl.multiple_of((blk - 1) * BLOCK, BLOCK)
            bias = bias_ref[0]
        k = kd_ref[pl.ds(k0, 2 * BLOCK), :]
        v = vd_ref[pl.ds(k0, 2 * BLOCK), :]
        q = q_ref[0, pl.ds(q0, BLOCK), :] * QK_SCALE
        stacked = []
        for j in range(B_GROUP):
            qp = q[:, (j // 2) * LANES:(j // 2 + 1) * LANES]
            stacked.append(jnp.where(head0 if j % 2 == 0 else ~head0, qp, jnp.zeros_like(qp)))
        s_all = _dot_nt(jnp.concatenate(stacked, axis=0), k) + bias
        probs, inv = [], []
        for j in range(B_GROUP):
            s = s_all[j * BLOCK:(j + 1) * BLOCK]
            m = jnp.maximum(jnp.max(s, axis=-1, keepdims=True), sinks[j])
            e = jnp.exp(s - m)
            inv.append(1.0 / (jnp.sum(e, axis=-1, keepdims=True) + jnp.exp(sinks[j] - m)))
            probs.append(e.astype(BF16))
        pv = _dot(jnp.concatenate(probs, axis=0), v)
        for p in range(B_GROUP // 2):
            even = pv[(2 * p) * BLOCK:(2 * p + 1) * BLOCK]
            odd = pv[(2 * p + 1) * BLOCK:(2 * p + 2) * BLOCK]
            o_ref[0, pl.ds(q0, BLOCK), p * LANES:(p + 1) * LANES] = (
                jnp.where(head0, even, odd) * jnp.where(head0, inv[2 * p], inv[2 * p + 1]))

    def body(it, carry):
        for i in range(ATTN_B_UNITS_PER_ITER):
            unit(it * ATTN_B_UNITS_PER_ITER + i, i == 0)
        return carry

    lax.fori_loop(0, seq // BLOCK // ATTN_B_UNITS_PER_ITER, body, 0)


def _attn_b(slopes, sinks, zb):
    bsz, seq, _ = zb.shape
    qw = B_GROUP * HEAD_DIM
    k_blk = B_WIDTH // LANES
    return pl.pallas_call(
        _attn_b_kernel,
        grid=(bsz, B_KV_HEADS),
        in_specs=[
            pl.BlockSpec(memory_space=pltpu.SMEM),
            pl.BlockSpec(memory_space=pltpu.SMEM),
            pl.BlockSpec((1, seq, qw), lambda b, g: (b, 0, g)),
            pl.BlockSpec((1, seq, LANES), lambda b, g: (b, 0, k_blk)),
            pl.BlockSpec((1, seq, LANES), lambda b, g: (b, 0, k_blk + 1)),
        ],
        out_specs=pl.BlockSpec((1, seq, qw), lambda b, g: (b, 0, g)),
        out_shape=jax.ShapeDtypeStruct((bsz, seq, B_WIDTH), F32),
        scratch_shapes=[pltpu.VMEM((seq, LANES), BF16)] * 2 + [
            pltpu.VMEM((2, B_GROUP * BLOCK, 2 * BLOCK), F32)],
        compiler_params=_params("parallel", "parallel"),
        name="attn_b",
    )(slopes, sinks, zb, zb, zb)


def _out_kernel(h_ref, oa_ref, ob_ref, ga_ref, gb_ref, wo_ref, bo_ref, gpost_ref, o_ref):
    na = _rms(oa_ref[...], ga_ref[...]).astype(BF16)
    nb = _rms(ob_ref[...], gb_ref[...]).astype(BF16)
    o = _dot(na, wo_ref[0:A_WIDTH, :]) + _dot(nb, wo_ref[A_WIDTH:A_WIDTH + B_WIDTH, :]) + bo_ref[...]
    o_ref[...] = h_ref[...] + _rms(o, gpost_ref[...])


def _out_proj(h, oa, ob, g_a, g_b, w_o, b_o, g_post):
    t, d = h.shape
    tm = OUT_TM
    row = lambda i: (i, 0)
    return pl.pallas_call(
        _out_kernel,
        grid=(t // tm,),
        in_specs=[
            pl.BlockSpec((tm, d), row),
            pl.BlockSpec((tm, A_WIDTH), row),
            pl.BlockSpec((tm, B_WIDTH), row),
            _resident((1, A_WIDTH)),
            _resident((1, B_WIDTH)),
            _resident((A_WIDTH + B_WIDTH, d)),
            _resident((1, d)),
            _resident((1, d)),
        ],
        out_specs=pl.BlockSpec((tm, d), row),
        out_shape=jax.ShapeDtypeStruct((t, d), F32),
        compiler_params=_params("parallel"),
        name="out_proj",
    )(h, oa, ob, g_a, g_b, w_o, b_o, g_post)


def _ple_kernel(h_ref, p_ref, gpre_ref, wg_ref, wp_ref, gpost_ref, o_ref):
    h = h_ref[...]
    gate = jax.nn.sigmoid(_dot(_rms(h, gpre_ref[...]).astype(BF16), wg_ref[...]))
    e = _dot(p_ref[...].astype(BF16), wp_ref[...])
    o_ref[...] = h + _rms(gate * e, gpost_ref[...])


def _ple(h, p, g_pre, w_gate, w_proj, g_post):
    t, d = h.shape
    pd = p.shape[1]
    tm = PLE_TM
    row = lambda i: (i, 0)
    return pl.pallas_call(
        _ple_kernel,
        grid=(t // tm,),
        in_specs=[
            pl.BlockSpec((tm, d), row),
            pl.BlockSpec((tm, pd), row),
            _resident((1, d)),
            _resident((d, d)),
            _resident((pd, d)),
            _resident((1, d)),
        ],
        out_specs=pl.BlockSpec((tm, d), row),
        out_shape=jax.ShapeDtypeStruct((t, d), F32),
        compiler_params=_params("parallel"),
        name="ple",
    )(h, p, g_pre, w_gate, w_proj, g_post)


def _alibi_slopes(n):
    return jnp.exp2(-ALIBI_MAX_BIAS * (jnp.arange(n, dtype=F32) + 1.0) / n)


def kernel(x, p, g_ffn1_pre, w_ffn1_gate, w_ffn1_up, w_ffn1_down, g_ffn1_post, g_mix_pre, w_qkv, b_qkv, attn_sinks, g_out_a, g_out_b, w_o, b_o, g_mix_post, g_ffn2_pre, w_ffn2_gate, w_ffn2_up, w_ffn2_down, g_ffn2_post, g_ple_pre, w_ple_gate, w_ple_proj, g_ple_post):
    bsz, seq, d = x.shape
    depth = p.shape[0]
    t = bsz * seq
    slopes = _alibi_slopes(A_HEADS + B_Q_HEADS)
    bf = lambda w: w.astype(BF16)
    h = x.reshape(t, d)
    for i in range(depth):
        h = _ffn(h, g_ffn1_pre[i][None], w_ffn1_gate[i], w_ffn1_up[i], w_ffn1_down[i], g_ffn1_post[i][None])
        nat, a4, a16, zb = _qkv(h, g_mix_pre[i][None], bf(w_qkv[i]), b_qkv[i][None], bsz, seq)
        oa = _attn_a(slopes, nat.reshape(-1, bsz, seq, LANES), a4, a16)
        ob = _attn_b(slopes, attn_sinks[i].reshape(-1), zb.reshape(bsz, seq, ZB_WIDTH))
        h = _out_proj(h, oa.reshape(t, A_WIDTH), ob.reshape(t, B_WIDTH), g_out_a[i][None], g_out_b[i][None],
                      bf(w_o[i]), b_o[i][None], g_mix_post[i][None])
        h = _ffn(h, g_ffn2_pre[i][None], w_ffn2_gate[i], w_ffn2_up[i], w_ffn2_down[i], g_ffn2_post[i][None])
        h = _ple(h, p[i].reshape(t, -1), g_ple_pre[i][None], bf(w_ple_gate[i]), bf(w_ple_proj[i]),
                 g_ple_post[i][None])
    return h.reshape(bsz, seq, d)
```

```python
import jax
import jax.numpy as jnp
from jax import lax
from jax.experimental import pallas as pl
from jax.experimental.pallas import tpu as pltpu

F32 = jnp.float32
BF16 = jnp.bfloat16

HEAD_DIM = 64
LANES = 128
MXU_WIDTH = 256
BLOCK = 128
A_HEADS = 16
B_Q_HEADS = 16
B_KV_HEADS = 2
B_GROUP = B_Q_HEADS // B_KV_HEADS
A_WIDTH = A_HEADS * HEAD_DIM
B_WIDTH = B_Q_HEADS * HEAD_DIM
KV_B_WIDTH = B_KV_HEADS * HEAD_DIM
ZB_WIDTH = B_WIDTH + 2 * KV_B_WIDTH
DILATIONS = (1, 4, 16)
A_WINDOW = 128
B_WINDOW = 127
ALIBI_MAX_BIAS = 8.0
EPS = 1e-6
QK_SCALE = HEAD_DIM ** -0.5
NEG_INF = float("-inf")

VMEM_LIMIT = 61 * 1024 * 1024

NORM_ROWS = 256
FFN_TM = 1024
FFN_TF = 512
FFN_SPLIT = 2
FFN_HEAD_TF = 256
QKV_TM = 512
ATTN_A_UNITS_PER_ITER = 16
ATTN_B_UNITS_PER_ITER = 8
OUT_TM = 512
PLE_TM = 1024


def _rms(x, g):
    return x * lax.rsqrt(jnp.mean(x * x, axis=-1, keepdims=True) + EPS) * g


def _dot(a, b):
    return jnp.dot(a, b, preferred_element_type=F32)


def _dot_nt(a, b):
    return lax.dot_general(a, b, (((1,), (1,)), ((), ())), preferred_element_type=F32)


def _resident(shape):
    return pl.BlockSpec(shape, lambda *_: (0,) * len(shape), pipeline_mode=pl.Buffered(1))


def _params(*semantics):
    return pltpu.CompilerParams(dimension_semantics=semantics, vmem_limit_bytes=VMEM_LIMIT)


def _ffn_steps(x_ref, gpre_ref, weights, gpost_ref, o_ref, xn_ref, active=True):
    j = pl.program_id(1)
    last = pl.num_programs(1) - 1
    tm = x_ref.shape[0]
    pieces = [pl.ds(i * (tm // FFN_SPLIT), tm // FFN_SPLIT) for i in range(FFN_SPLIT)]

    def prenorm(rows):
        for c in range(rows.size // NORM_ROWS):
            r = pl.ds(rows.start + c * NORM_ROWS, NORM_ROWS)
            xn_ref[r, :] = _rms(x_ref[r, :], gpre_ref[...]).astype(BF16)

    def postnorm(rows):
        for c in range(rows.size // NORM_ROWS):
            r = pl.ds(rows.start + c * NORM_ROWS, NORM_ROWS)
            o_ref[r, :] = x_ref[r, :] + 0.5 * _rms(o_ref[r, :], gpost_ref[...])

    def swiglu(w, rows, accumulate):
        wg, wu, wd = w
        xn = xn_ref[rows, :]
        gate = _dot(xn, wg)
        up = _dot(xn, wu)
        down = _dot((gate * jax.nn.sigmoid(gate) * up).astype(BF16), wd)
        o_ref[rows, :] = o_ref[rows, :] + down if accumulate else down

    @pl.when((j == 0) & active)
    def _():
        w = weights()
        for rows in pieces:
            prenorm(rows)
        for rows in pieces:
            swiglu(w, rows, False)

    @pl.when((j > 0) & (j < last) & active)
    def _():
        swiglu(weights(), pl.ds(0, tm), True)

    @pl.when((j == last) & active)
    def _():
        w = weights()
        for rows in pieces:
            swiglu(w, rows, True)
        for rows in pieces:
            postnorm(rows)


def _ffn_head_kernel(x_ref, gpre_ref, wg32_ref, wu32_ref, wd32_ref, gpost_ref,
                     o_ref, wg_ref, wu_ref, wd_ref, xn_ref):
    def weights():
        w = [r[...].astype(BF16) for r in (wg32_ref, wu32_ref, wd32_ref)]
        for dst, val in zip((wg_ref, wu_ref, wd_ref), w):
            dst[...] = val
        return w

    _ffn_steps(x_ref, gpre_ref, weights, gpost_ref, o_ref, xn_ref)


def _ffn_main_kernel(x_ref, gpre_ref, wg_ref, wu_ref, wd_ref, gpost_ref, o_ref, xn_ref):
    i = pl.program_id(0)

    @pl.when((i == 0) & (pl.program_id(1) == 0))
    def _():
        o_ref[...] = jnp.zeros_like(o_ref)

    _ffn_steps(x_ref, gpre_ref, lambda: (wg_ref[...], wu_ref[...], wd_ref[...]), gpost_ref, o_ref, xn_ref,
               active=i > 0)


def _ffn(h, g_pre, w_gate, w_up, w_down, g_post):
    t, d = h.shape
    d_ff = w_gate.shape[1]

    tm, tf = FFN_TM, FFN_HEAD_TF
    head_out, wg, wu, wd = pl.pallas_call(
        _ffn_head_kernel,
        grid=(1, d_ff // tf),
        in_specs=[
            pl.BlockSpec((tm, d), lambda i, j: (0, 0), pipeline_mode=pl.Buffered(1)),
            _resident((1, d)),
            pl.BlockSpec((d, tf), lambda i, j: (0, j)),
            pl.BlockSpec((d, tf), lambda i, j: (0, j)),
            pl.BlockSpec((tf, d), lambda i, j: (j, 0)),
            _resident((1, d)),
        ],
        out_specs=[
            pl.BlockSpec((tm, d), lambda i, j: (0, 0)),
            pl.BlockSpec((d, tf), lambda i, j: (0, j)),
            pl.BlockSpec((d, tf), lambda i, j: (0, j)),
            pl.BlockSpec((tf, d), lambda i, j: (j, 0)),
        ],
        out_shape=[jax.ShapeDtypeStruct((tm, d), F32), jax.ShapeDtypeStruct(w_gate.shape, BF16),
                   jax.ShapeDtypeStruct(w_up.shape, BF16), jax.ShapeDtypeStruct(w_down.shape, BF16)],
        scratch_shapes=[pltpu.VMEM((tm, d), BF16)],
        compiler_params=_params("arbitrary", "arbitrary"),
        name="ffn_head",
    )(h, g_pre, w_gate, w_up, w_down, g_post)

    tf = FFN_TF

    def step(i, j):
        return jnp.where(i == 0, 0, j)

    rest = pl.pallas_call(
        _ffn_main_kernel,
        grid=(t // tm, d_ff // tf),
        in_specs=[
            pl.BlockSpec((tm, d), lambda i, j: (jnp.maximum(i, 1), 0)),
            _resident((1, d)),
            pl.BlockSpec((d, tf), lambda i, j: (0, step(i, j))),
            pl.BlockSpec((d, tf), lambda i, j: (0, step(i, j))),
            pl.BlockSpec((tf, d), lambda i, j: (step(i, j), 0)),
            _resident((1, d)),
        ],
        out_specs=pl.BlockSpec((tm, d), lambda i, j: (i, 0)),
        out_shape=jax.ShapeDtypeStruct((t, d), F32),
        scratch_shapes=[pltpu.VMEM((tm, d), BF16)],
        compiler_params=_params("parallel", "arbitrary"),
        name="ffn",
    )(h, g_pre, wg, wu, wd, g_post)
    return lax.dynamic_update_slice(rest, head_out, (0, 0))


def _qkv_kernel(h_ref, g_ref, w_ref, b_ref, nat_ref, a4_ref, a16_ref, zb_ref, zs_ref, z4_ref):
    tm = h_ref.shape[0]
    nc = zs_ref.shape[0]
    per_chunk = MXU_WIDTH // LANES
    xn = _rms(h_ref[...], g_ref[...]).astype(BF16)
    zb_ref[...] = (_dot(xn, w_ref[:, nc * LANES:]) + b_ref[:, nc * LANES:]).astype(BF16)
    for c2 in range(nc // per_chunk):
        wide = slice(c2 * MXU_WIDTH, (c2 + 1) * MXU_WIDTH)
        z2 = _dot(xn, w_ref[:, wide]) + b_ref[:, wide]
        for half in range(per_chunk):
            c = c2 * per_chunk + half
            zc = z2[:, half * LANES:(half + 1) * LANES]
            zs_ref[c] = zc
            nat_ref[c] = zc.astype(BF16)

    def copy(c, carry):
        for r in range(4):
            z4 = zs_ref[c, pl.ds(r, tm // 4, stride=4), :]
            a4_ref[0, c, r] = z4.astype(BF16)
            z4_ref[r] = z4
        for r in range(16):
            a16_ref[0, c, r] = z4_ref[r % 4, pl.ds(r // 4, tm // 16, stride=4), :].astype(BF16)
        return carry

    lax.fori_loop(0, nc, copy, 0)


def _qkv(h, g, w, b, bsz, seq):
    t, d = h.shape
    tm = QKV_TM
    nc = 3 * A_WIDTH // LANES
    per_seq = seq // tm
    return pl.pallas_call(
        _qkv_kernel,
        grid=(t // tm,),
        in_specs=[
            pl.BlockSpec((tm, d), lambda i: (i, 0)),
            _resident((1, d)),
            _resident(w.shape),
            _resident(b.shape),
        ],
        out_specs=[
            pl.BlockSpec((nc, tm, LANES), lambda i: (0, i, 0)),
            pl.BlockSpec((1, nc, 4, tm // 4, LANES), lambda i: (i // per_seq, 0, 0, i % per_seq, 0)),
            pl.BlockSpec((1, nc, 16, tm // 16, LANES), lambda i: (i // per_seq, 0, 0, i % per_seq, 0)),
            pl.BlockSpec((tm, ZB_WIDTH), lambda i: (i, 0)),
        ],
        out_shape=[
            jax.ShapeDtypeStruct((nc, t, LANES), BF16),
            jax.ShapeDtypeStruct((bsz, nc, 4, seq // 4, LANES), BF16),
            jax.ShapeDtypeStruct((bsz, nc, 16, seq // 16, LANES), BF16),
            jax.ShapeDtypeStruct((t, ZB_WIDTH), BF16),
        ],
        scratch_shapes=[pltpu.VMEM((nc, tm, LANES), F32), pltpu.VMEM((4, tm // 4, LANES), F32)],
        compiler_params=_params("parallel"),
        name="qkv",
    )(h, g, w, b)


def _attn_a_kernel(slopes_ref,
                   q1_ref, k1_ref, v1_ref, q4_ref, k4_ref, v4_ref, q16_ref, k16_ref, v16_ref,
                   o_ref, acc_ref, m_ref, l_ref, bias_ref, bias16_ref):
    hp = pl.program_id(1)
    seq = q1_ref.shape[1]
    lane = lax.broadcasted_iota(jnp.int32, (1, LANES), 1)
    head0 = lane < HEAD_DIM
    neg_slopes = (-slopes_ref[4 * hp], -slopes_ref[4 * hp + 2])

    def bias_table(nk, lead, dil):
        qi = lax.broadcasted_iota(jnp.int32, (BLOCK, nk), 0)
        kj = lax.broadcasted_iota(jnp.int32, (BLOCK, nk), 1)
        rel = qi - kj if lead else qi + (nk - BLOCK) - kj
        valid = (rel >= 0) & (rel <= A_WINDOW)
        rel_f = rel.astype(F32)
        return jnp.concatenate(
            [jnp.where(valid, rel_f * (ns * float(dil)), NEG_INF) for ns in neg_slopes], axis=0)

    for bi, dil in enumerate(DILATIONS[:2]):
        bias_ref[bi, 0] = bias_table(2 * BLOCK, False, dil)
        bias_ref[bi, 1] = bias_table(2 * BLOCK, True, dil)
    bias16_ref[...] = bias_table(BLOCK, True, DILATIONS[2])

    def unit(q, k, v, bias):
        zero = jnp.zeros_like(q)
        qs = jnp.concatenate([jnp.where(head0, q, zero), jnp.where(head0, zero, q)], axis=0)
        s = _dot_nt(qs * QK_SCALE, k) + bias
        m = jnp.max(s, axis=-1, keepdims=True)
        e = jnp.exp(s - m)
        l = jnp.sum(e, axis=-1, keepdims=True)
        pv = _dot(e.astype(BF16), v)
        return (jnp.where(head0, pv[:BLOCK], pv[BLOCK:]),
                jnp.where(head0, m[:BLOCK], m[BLOCK:]),
                jnp.where(head0, l[:BLOCK], l[BLOCK:]))

    def store(branch, rows, res):
        acc_ref[branch, rows, :] = res[0]
        m_ref[branch, rows, :] = res[1]
        l_ref[branch, rows, :] = res[2]

    unroll = ATTN_A_UNITS_PER_ITER
    nb4 = seq // 4 // BLOCK

    def body1(g, carry):
        for i in range(unroll):
            blk = g * unroll + i
            q0 = pl.multiple_of(blk * BLOCK, BLOCK)
            if i == 0:
                k0 = pl.multiple_of(jnp.maximum(blk - 1, 0) * BLOCK, BLOCK)
                bias = bias_ref[0, jnp.where(blk == 0, 1, 0)]
            else:
                k0 = pl.multiple_of((blk - 1) * BLOCK, BLOCK)
                bias = bias_ref[0, 0]
            store(0, pl.ds(q0, BLOCK),
                  unit(q1_ref[0, pl.ds(q0, BLOCK), :], k1_ref[0, pl.ds(k0, 2 * BLOCK), :],
                       v1_ref[0, pl.ds(k0, 2 * BLOCK), :], bias))
        return carry

    lax.fori_loop(0, seq // BLOCK // unroll, body1, 0)

    def body4(g, carry):
        for i in range(unroll):
            r = g * (unroll // nb4) + i // nb4
            blk = i % nb4
            k0 = max(blk - 1, 0) * BLOCK
            store(1, pl.ds(r + 4 * blk * BLOCK, BLOCK, stride=4),
                  unit(q4_ref[0, r, pl.ds(blk * BLOCK, BLOCK), :],
                       k4_ref[0, r, pl.ds(k0, 2 * BLOCK), :],
                       v4_ref[0, r, pl.ds(k0, 2 * BLOCK), :], bias_ref[1, 1 if blk == 0 else 0]))
        return carry

    lax.fori_loop(0, 4 * nb4 // unroll, body4, 0)

    def body16(g, carry):
        for i in range(unroll):
            r = g * unroll + i
            store(2, pl.ds(r, BLOCK, stride=16),
                  unit(q16_ref[0, r], k16_ref[0, r], v16_ref[0, r], bias16_ref[...]))
        return carry

    lax.fori_loop(0, 16 // unroll, body16, 0)

    def merge(blk, carry):
        rows = pl.ds(pl.multiple_of(blk * (2 * BLOCK), 2 * BLOCK), 2 * BLOCK)
        m0, m1, m2 = m_ref[0, rows, :], m_ref[1, rows, :], m_ref[2, rows, :]
        mx = jnp.maximum(jnp.maximum(m0, m1), m2)
        w0, w1, w2 = jnp.exp(m0 - mx), jnp.exp(m1 - mx), jnp.exp(m2 - mx)
        num = w0 * acc_ref[0, rows, :] + w1 * acc_ref[1, rows, :] + w2 * acc_ref[2, rows, :]
        den = w0 * l_ref[0, rows, :] + w1 * l_ref[1, rows, :] + w2 * l_ref[2, rows, :]
        o_ref[0, rows, :] = num / den
        return carry

    lax.fori_loop(0, seq // (2 * BLOCK), merge, 0)


def _attn_a(slopes, nat, a4, a16):
    _, bsz, seq, _ = nat.shape
    pairs = A_WIDTH // LANES

    def col(base):
        return lambda b, hp: (base + hp, b, 0, 0)

    def col4(base):
        return lambda b, hp: (b, base + hp, 0, 0, 0)

    nat_spec = [pl.BlockSpec((None, 1, seq, LANES), col(i * pairs)) for i in range(3)]
    a4_spec = [pl.BlockSpec((1, None, 4, seq // 4, LANES), col4(i * pairs)) for i in range(3)]
    a16_spec = [pl.BlockSpec((1, None, 16, seq // 16, LANES), col4(i * pairs)) for i in range(3)]
    return pl.pallas_call(
        _attn_a_kernel,
        grid=(bsz, pairs),
        in_specs=[pl.BlockSpec(memory_space=pltpu.SMEM)] + nat_spec + a4_spec + a16_spec,
        out_specs=pl.BlockSpec((1, seq, LANES), lambda b, hp: (b, 0, hp)),
        out_shape=jax.ShapeDtypeStruct((bsz, seq, A_WIDTH), F32),
        scratch_shapes=[pltpu.VMEM((3, seq, LANES), F32)] * 3 + [
            pltpu.VMEM((2, 2, 2 * BLOCK, 2 * BLOCK), F32), pltpu.VMEM((2 * BLOCK, BLOCK), F32)],
        compiler_params=_params("parallel", "parallel"),
        name="attn_a",
    )(slopes, nat, nat, nat, a4, a4, a4, a16, a16, a16)


def _attn_b_kernel(slopes_ref, sinks_ref, q_ref, k_ref, v_ref, o_ref, kd_ref, vd_ref, bias_ref):
    g = pl.program_id(1)
    seq = q_ref.shape[1]
    lane = lax.broadcasted_iota(jnp.int32, (1, LANES), 1)
    head0 = lane < HEAD_DIM
    mine = (lane >= HEAD_DIM).astype(jnp.int32) == g

    def both_halves(ref):
        x = ref[0].astype(F32)
        return jnp.where(mine, x, pltpu.roll(x, HEAD_DIM, 1)).astype(BF16)

    kd_ref[...] = both_halves(k_ref)
    vd_ref[...] = both_halves(v_ref)

    neg_slopes = [-slopes_ref[2 * (g * B_GROUP + j) + 1] for j in range(B_GROUP)]
    sinks = [sinks_ref[g * B_GROUP + j] for j in range(B_GROUP)]

    for lead in (False, True):
        qi = lax.broadcasted_iota(jnp.int32, (BLOCK, 2 * BLOCK), 0)
        kj = lax.broadcasted_iota(jnp.int32, (BLOCK, 2 * BLOCK), 1)
        rel = qi - kj if lead else qi + BLOCK - kj
        valid = (rel >= 0) & (rel <= B_WINDOW)
        rel_f = rel.astype(F32)
        for j in range(B_GROUP):
            bias_ref[int(lead), pl.ds(j * BLOCK, BLOCK), :] = jnp.where(valid, rel_f * neg_slopes[j], NEG_INF)

    def unit(blk, lead_possible):
        q0 = pl.multiple_of(blk * BLOCK, BLOCK)
        if lead_possible:
            k0 = pl.multiple_of(jnp.maximum(blk - 1, 0) * BLOCK, BLOCK)
            bias = bias_ref[jnp.where(blk == 0, 1, 0)]
        else:
            k0 = pl.multiple_of((blk - 1) * BLOCK, BLOCK)
            bias = bias_ref[0]
        k = kd_ref[pl.ds(k0, 2 * BLOCK), :]
        v = vd_ref[pl.ds(k0, 2 * BLOCK), :]
        q = q_ref[0, pl.ds(q0, BLOCK), :] * QK_SCALE
        stacked = []
        for j in range(B_GROUP):
            qp = q[:, (j // 2) * LANES:(j // 2 + 1) * LANES]
            stacked.append(jnp.where(head0 if j % 2 == 0 else ~head0, qp, jnp.zeros_like(qp)))
        s_all = _dot_nt(jnp.concatenate(stacked, axis=0), k) + bias
        probs, inv = [], []
        for j in range(B_GROUP):
            s = s_all[j * BLOCK:(j + 1) * BLOCK]
            m = jnp.maximum(jnp.max(s, axis=-1, keepdims=True), sinks[j])
            e = jnp.exp(s - m)
            inv.append(1.0 / (jnp.sum(e, axis=-1, keepdims=True) + jnp.exp(sinks[j] - m)))
            probs.append(e.astype(BF16))
        pv = _dot(jnp.concatenate(probs, axis=0), v)
        for p in range(B_GROUP // 2):
            even = pv[(2 * p) * BLOCK:(2 * p + 1) * BLOCK]
            odd = pv[(2 * p + 1) * BLOCK:(2 * p + 2) * BLOCK]
            o_ref[0, pl.ds(q0, BLOCK), p * LANES:(p + 1) * LANES] = (
                jnp.where(head0, even, odd) * jnp.where(head0, inv[2 * p], inv[2 * p + 1]))

    def body(it, carry):
        for i in range(ATTN_B_UNITS_PER_ITER):
            unit(it * ATTN_B_UNITS_PER_ITER + i, i == 0)
        return carry

    lax.fori_loop(0, seq // BLOCK // ATTN_B_UNITS_PER_ITER, body, 0)


def _attn_b(slopes, sinks, zb):
    bsz, seq, _ = zb.shape
    qw = B_GROUP * HEAD_DIM
    k_blk = B_WIDTH // LANES
    return pl.pallas_call(
        _attn_b_kernel,
        grid=(bsz, B_KV_HEADS),
        in_specs=[
            pl.BlockSpec(memory_space=pltpu.SMEM),
            pl.BlockSpec(memory_space=pltpu.SMEM),
            pl.BlockSpec((1, seq, qw), lambda b, g: (b, 0, g)),
            pl.BlockSpec((1, seq, LANES), lambda b, g: (b, 0, k_blk)),
            pl.BlockSpec((1, seq, LANES), lambda b, g: (b, 0, k_blk + 1)),
        ],
        out_specs=pl.BlockSpec((1, seq, qw), lambda b, g: (b, 0, g)),
        out_shape=jax.ShapeDtypeStruct((bsz, seq, B_WIDTH), F32),
        scratch_shapes=[pltpu.VMEM((seq, LANES), BF16)] * 2 + [
            pltpu.VMEM((2, B_GROUP * BLOCK, 2 * BLOCK), F32)],
        compiler_params=_params("parallel", "parallel"),
        name="attn_b",
    )(slopes, sinks, zb, zb, zb)


def _out_kernel(h_ref, oa_ref, ob_ref, ga_ref, gb_ref, wo_ref, bo_ref, gpost_ref, o_ref):
    na = _rms(oa_ref[...], ga_ref[...]).astype(BF16)
    nb = _rms(ob_ref[...], gb_ref[...]).astype(BF16)
    o = _dot(na, wo_ref[0:A_WIDTH, :]) + _dot(nb, wo_ref[A_WIDTH:A_WIDTH + B_WIDTH, :]) + bo_ref[...]
    o_ref[...] = h_ref[...] + _rms(o, gpost_ref[...])


def _out_proj(h, oa, ob, g_a, g_b, w_o, b_o, g_post):
    t, d = h.shape
    tm = OUT_TM
    row = lambda i: (i, 0)
    return pl.pallas_call(
        _out_kernel,
        grid=(t // tm,),
        in_specs=[
            pl.BlockSpec((tm, d), row),
            pl.BlockSpec((tm, A_WIDTH), row),
            pl.BlockSpec((tm, B_WIDTH), row),
            _resident((1, A_WIDTH)),
            _resident((1, B_WIDTH)),
            _resident((A_WIDTH + B_WIDTH, d)),
            _resident((1, d)),
            _resident((1, d)),
        ],
        out_specs=pl.BlockSpec((tm, d), row),
        out_shape=jax.ShapeDtypeStruct((t, d), F32),
        compiler_params=_params("parallel"),
        name="out_proj",
    )(h, oa, ob, g_a, g_b, w_o, b_o, g_post)


def _ple_kernel(h_ref, p_ref, gpre_ref, wg_ref, wp_ref, gpost_ref, o_ref):
    h = h_ref[...]
    gate = jax.nn.sigmoid(_dot(_rms(h, gpre_ref[...]).astype(BF16), wg_ref[...]))
    e = _dot(p_ref[...].astype(BF16), wp_ref[...])
    o_ref[...] = h + _rms(gate * e, gpost_ref[...])


def _ple(h, p, g_pre, w_gate, w_proj, g_post):
    t, d = h.shape
    pd = p.shape[1]
    tm = PLE_TM
    row = lambda i: (i, 0)
    return pl.pallas_call(
        _ple_kernel,
        grid=(t // tm,),
        in_specs=[
            pl.BlockSpec((tm, d), row),
            pl.BlockSpec((tm, pd), row),
            _resident((1, d)),
            _resident((d, d)),
            _resident((pd, d)),
            _resident((1, d)),
        ],
        out_specs=pl.BlockSpec((tm, d), row),
        out_shape=jax.ShapeDtypeStruct((t, d), F32),
        compiler_params=_params("parallel"),
        name="ple",
    )(h, p, g_pre, w_gate, w_proj, g_post)


def _alibi_slopes(n):
    return jnp.exp2(-ALIBI_MAX_BIAS * (jnp.arange(n, dtype=F32) + 1.0) / n)


def kernel(x, p, g_ffn1_pre, w_ffn1_gate, w_ffn1_up, w_ffn1_down, g_ffn1_post, g_mix_pre, w_qkv, b_qkv, attn_sinks, g_out_a, g_out_b, w_o, b_o, g_mix_post, g_ffn2_pre, w_ffn2_gate, w_ffn2_up, w_ffn2_down, g_ffn2_post, g_ple_pre, w_ple_gate, w_ple_proj, g_ple_post):
    bsz, seq, d = x.shape
    depth = p.shape[0]
    t = bsz * seq
    slopes = _alibi_slopes(A_HEADS + B_Q_HEADS)
    bf = lambda w: w.astype(BF16)
    h = x.reshape(t, d)
    for i in range(depth):
        h = _ffn(h, g_ffn1_pre[i][None], w_ffn1_gate[i], w_ffn1_up[i], w_ffn1_down[i], g_ffn1_post[i][None])
        nat, a4, a16, zb = _qkv(h, g_mix_pre[i][None], bf(w_qkv[i]), b_qkv[i][None], bsz, seq)
        oa = _attn_a(slopes, nat.reshape(-1, bsz, seq, LANES), a4, a16)
        ob = _attn_b(slopes, attn_sinks[i].reshape(-1), zb.reshape(bsz, seq, ZB_WIDTH))
        h = _out_proj(h, oa.reshape(t, A_WIDTH), ob.reshape(t, B_WIDTH), g_out_a[i][None], g_out_b[i][None],
                      bf(w_o[i]), b_o[i][None], g_mix_post[i][None])
        h = _ffn(h, g_ffn2_pre[i][None], w_ffn2_gate[i], w_ffn2_up[i], w_ffn2_down[i], g_ffn2_post[i][None])
        h = _ple(h, p[i].reshape(t, -1), g_ple_pre[i][None], bf(w_ple_gate[i]), bf(w_ple_proj[i]),
                 g_ple_post[i][None])
    return h.reshape(bsz, seq, d)
```

```python
import functools

import jax
import jax.numpy as jnp
from jax import lax
from jax.experimental import pallas as pl
from jax.experimental.pallas import tpu as pltpu

F32 = jnp.float32
BF16 = jnp.bfloat16

HEAD_DIM = 64
LANES = 128
MXU_WIDTH = 256
BLOCK = 128
A_HEADS = 16
B_Q_HEADS = 16
B_KV_HEADS = 2
B_GROUP = B_Q_HEADS // B_KV_HEADS
A_WIDTH = A_HEADS * HEAD_DIM
B_WIDTH = B_Q_HEADS * HEAD_DIM
KV_B_WIDTH = B_KV_HEADS * HEAD_DIM
ZB_WIDTH = B_WIDTH + 2 * KV_B_WIDTH
DILATIONS = (1, 4, 16)
A_WINDOW = 128
B_WINDOW = 127
ALIBI_MAX_BIAS = 8.0
EPS = 1e-6
QK_SCALE = HEAD_DIM ** -0.5
NEG_INF = float("-inf")

VMEM_LIMIT = 61 * 1024 * 1024

NORM_ROWS = 256
FFN_TM = 1024
FFN_TF = 512
FFN_SPLIT = 2
FFN_HEAD_TF = 256
QKV_TM = 512
ATTN_A_UNITS_PER_ITER = 16
ATTN_B_UNITS_PER_ITER = 8
OUT_TM = 512
PLE_TM = 1024


def _rms(x, g):
    return x * lax.rsqrt(jnp.mean(x * x, axis=-1, keepdims=True) + EPS) * g


def _dot(a, b):
    return jnp.dot(a, b, preferred_element_type=F32)


def _dot_nt(a, b):
    return lax.dot_general(a, b, (((1,), (1,)), ((), ())), preferred_element_type=F32)


def _resident(shape):
    return pl.BlockSpec(shape, lambda *_: (0,) * len(shape), pipeline_mode=pl.Buffered(1))


def _params(*semantics):
    return pltpu.CompilerParams(dimension_semantics=semantics, vmem_limit_bytes=VMEM_LIMIT)


def _ffn_steps(j, n_steps, active, x_ref, gpre_ref, weights, gpost_ref, o_ref, xn_ref):
    last = n_steps - 1
    tm = x_ref.shape[0]
    pieces = [pl.ds(i * (tm // FFN_SPLIT), tm // FFN_SPLIT) for i in range(FFN_SPLIT)]

    def prenorm(rows):
        for c in range(rows.size // NORM_ROWS):
            r = pl.ds(rows.start + c * NORM_ROWS, NORM_ROWS)
            xn_ref[r, :] = _rms(x_ref[r, :], gpre_ref[...]).astype(BF16)

    def postnorm(rows):
        for c in range(rows.size // NORM_ROWS):
            r = pl.ds(rows.start + c * NORM_ROWS, NORM_ROWS)
            o_ref[r, :] = x_ref[r, :] + 0.5 * _rms(o_ref[r, :], gpost_ref[...])

    def swiglu(w, rows, accumulate):
        wg, wu, wd = w
        xn = xn_ref[rows, :]
        gate = _dot(xn, wg)
        up = _dot(xn, wu)
        down = _dot((gate * jax.nn.sigmoid(gate) * up).astype(BF16), wd)
        o_ref[rows, :] = o_ref[rows, :] + down if accumulate else down

    @pl.when((j == 0) & active)
    def _():
        w = weights()
        for rows in pieces:
            prenorm(rows)
        for rows in pieces:
            swiglu(w, rows, False)

    @pl.when((j > 0) & (j < last) & active)
    def _():
        swiglu(weights(), pl.ds(0, tm), True)

    @pl.when((j == last) & active)
    def _():
        w = weights()
        for rows in pieces:
            swiglu(w, rows, True)
        for rows in pieces:
            postnorm(rows)


def _ffn_head_kernel(x_ref, gpre_ref, wg32_ref, wu32_ref, wd32_ref, gpost_ref,
                     o_ref, wg_ref, wu_ref, wd_ref, xn_ref):
    def weights():
        w = [r[...].astype(BF16) for r in (wg32_ref, wu32_ref, wd32_ref)]
        for dst, val in zip((wg_ref, wu_ref, wd_ref), w):
            dst[...] = val
        return w

    _ffn_steps(pl.program_id(1), pl.num_programs(1), True,
               x_ref, gpre_ref, weights, gpost_ref, o_ref, xn_ref)


def _ffn_main_kernel(n_steps, x_ref, gpre_ref, wg_ref, wu_ref, wd_ref, gpost_ref, o_ref, xn_ref):
    s = pl.program_id(0)

    @pl.when(s == 0)
    def _():
        o_ref[...] = jnp.zeros_like(o_ref)

    _ffn_steps((s - 1) % n_steps, n_steps, s > 0,
               x_ref, gpre_ref, lambda: (wg_ref[...], wu_ref[...], wd_ref[...]), gpost_ref, o_ref, xn_ref)


def _ffn(h, g_pre, w_gate, w_up, w_down, g_post):
    t, d = h.shape
    d_ff = w_gate.shape[1]

    tm, tf = FFN_TM, FFN_HEAD_TF
    head_out, wg, wu, wd = pl.pallas_call(
        _ffn_head_kernel,
        grid=(1, d_ff // tf),
        in_specs=[
            pl.BlockSpec((tm, d), lambda i, j: (0, 0), pipeline_mode=pl.Buffered(1)),
            _resident((1, d)),
            pl.BlockSpec((d, tf), lambda i, j: (0, j)),
            pl.BlockSpec((d, tf), lambda i, j: (0, j)),
            pl.BlockSpec((tf, d), lambda i, j: (j, 0)),
            _resident((1, d)),
        ],
        out_specs=[
            pl.BlockSpec((tm, d), lambda i, j: (0, 0)),
            pl.BlockSpec((d, tf), lambda i, j: (0, j)),
            pl.BlockSpec((d, tf), lambda i, j: (0, j)),
            pl.BlockSpec((tf, d), lambda i, j: (j, 0)),
        ],
        out_shape=[jax.ShapeDtypeStruct((tm, d), F32), jax.ShapeDtypeStruct(w_gate.shape, BF16),
                   jax.ShapeDtypeStruct(w_up.shape, BF16), jax.ShapeDtypeStruct(w_down.shape, BF16)],
        scratch_shapes=[pltpu.VMEM((tm, d), BF16)],
        compiler_params=_params("arbitrary", "arbitrary"),
        name="ffn_head",
    )(h, g_pre, w_gate, w_up, w_down, g_post)

    tf = FFN_TF
    n_steps = d_ff // tf

    def tile(s):
        return 1 + jnp.maximum(s - 1, 0) // n_steps

    def step(s):
        return jnp.maximum(s - 1, 0) % n_steps

    rest = pl.pallas_call(
        functools.partial(_ffn_main_kernel, n_steps),
        grid=(1 + (t // tm - 1) * n_steps,),
        in_specs=[
            pl.BlockSpec((tm, d), lambda s: (tile(s), 0)),
            _resident((1, d)),
            pl.BlockSpec((d, tf), lambda s: (0, step(s))),
            pl.BlockSpec((d, tf), lambda s: (0, step(s))),
            pl.BlockSpec((tf, d), lambda s: (step(s), 0)),
            _resident((1, d)),
        ],
        out_specs=pl.BlockSpec((tm, d), lambda s: (jnp.where(s == 0, 0, tile(s)), 0)),
        out_shape=jax.ShapeDtypeStruct((t, d), F32),
        scratch_shapes=[pltpu.VMEM((tm, d), BF16)],
        compiler_params=_params("arbitrary"),
        name="ffn",
    )(h, g_pre, wg, wu, wd, g_post)
    return lax.dynamic_update_slice(rest, head_out, (0, 0))


def _qkv_kernel(h_ref, g_ref, w_ref, b_ref, nat_ref, a4_ref, a16_ref, zb_ref, zs_ref, z4_ref):
    tm = h_ref.shape[0]
    nc = zs_ref.shape[0]
    per_chunk = MXU_WIDTH // LANES
    xn = _rms(h_ref[...], g_ref[...]).astype(BF16)
    zb_ref[...] = (_dot(xn, w_ref[:, nc * LANES:]) + b_ref[:, nc * LANES:]).astype(BF16)
    for c2 in range(nc // per_chunk):
        wide = slice(c2 * MXU_WIDTH, (c2 + 1) * MXU_WIDTH)
        z2 = _dot(xn, w_ref[:, wide]) + b_ref[:, wide]
        for half in range(per_chunk):
            c = c2 * per_chunk + half
            zc = z2[:, half * LANES:(half + 1) * LANES]
            zs_ref[c] = zc
            nat_ref[c] = zc.astype(BF16)

    def copy(c, carry):
        for r in range(4):
            z4 = zs_ref[c, pl.ds(r, tm // 4, stride=4), :]
            a4_ref[0, c, r] = z4.astype(BF16)
            z4_ref[r] = z4
        for r in range(16):
            a16_ref[0, c, r] = z4_ref[r % 4, pl.ds(r // 4, tm // 16, stride=4), :].astype(BF16)
        return carry

    lax.fori_loop(0, nc, copy, 0)


def _qkv(h, g, w, b, bsz, seq):
    t, d = h.shape
    tm = QKV_TM
    nc = 3 * A_WIDTH // LANES
    per_seq = seq // tm
    return pl.pallas_call(
        _qkv_kernel,
        grid=(t // tm,),
        in_specs=[
            pl.BlockSpec((tm, d), lambda i: (i, 0)),
            _resident((1, d)),
            _resident(w.shape),
            _resident(b.shape),
        ],
        out_specs=[
            pl.BlockSpec((nc, tm, LANES), lambda i: (0, i, 0)),
            pl.BlockSpec((1, nc, 4, tm // 4, LANES), lambda i: (i // per_seq, 0, 0, i % per_seq, 0)),
            pl.BlockSpec((1, nc, 16, tm // 16, LANES), lambda i: (i // per_seq, 0, 0, i % per_seq, 0)),
            pl.BlockSpec((tm, ZB_WIDTH), lambda i: (i, 0)),
        ],
        out_shape=[
            jax.ShapeDtypeStruct((nc, t, LANES), BF16),
            jax.ShapeDtypeStruct((bsz, nc, 4, seq // 4, LANES), BF16),
            jax.ShapeDtypeStruct((bsz, nc, 16, seq // 16, LANES), BF16),
            jax.ShapeDtypeStruct((t, ZB_WIDTH), BF16),
        ],
        scratch_shapes=[pltpu.VMEM((nc, tm, LANES), F32), pltpu.VMEM((4, tm // 4, LANES), F32)],
        compiler_params=_params("parallel"),
        name="qkv",
    )(h, g, w, b)


def _attn_a_kernel(slopes_ref,
                   q1_ref, k1_ref, v1_ref, q4_ref, k4_ref, v4_ref, q16_ref, k16_ref, v16_ref,
                   o_ref, acc_ref, m_ref, l_ref, bias_ref, bias16_ref):
    hp = pl.program_id(1)
    seq = q1_ref.shape[1]
    lane = lax.broadcasted_iota(jnp.int32, (1, LANES), 1)
    head0 = lane < HEAD_DIM
    neg_slopes = (-slopes_ref[4 * hp], -slopes_ref[4 * hp + 2])

    def bias_table(nk, lead, dil):
        qi = lax.broadcasted_iota(jnp.int32, (BLOCK, nk), 0)
        kj = lax.broadcasted_iota(jnp.int32, (BLOCK, nk), 1)
        rel = qi - kj if lead else qi + (nk - BLOCK) - kj
        valid = (rel >= 0) & (rel <= A_WINDOW)
        rel_f = rel.astype(F32)
        return jnp.concatenate(
            [jnp.where(valid, rel_f * (ns * float(dil)), NEG_INF) for ns in neg_slopes], axis=0)

    for bi, dil in enumerate(DILATIONS[:2]):
        bias_ref[bi, 0] = bias_table(2 * BLOCK, False, dil)
        bias_ref[bi, 1] = bias_table(2 * BLOCK, True, dil)
    bias16_ref[...] = bias_table(BLOCK, True, DILATIONS[2])

    def unit(q, k, v, bias):
        zero = jnp.zeros_like(q)
        qs = jnp.concatenate([jnp.where(head0, q, zero), jnp.where(head0, zero, q)], axis=0)
        s = _dot_nt(qs * QK_SCALE, k) + bias
        m = jnp.max(s, axis=-1, keepdims=True)
        e = jnp.exp(s - m)
        l = jnp.sum(e, axis=-1, keepdims=True)
        pv = _dot(e.astype(BF16), v)
        return (jnp.where(head0, pv[:BLOCK], pv[BLOCK:]),
                jnp.where(head0, m[:BLOCK], m[BLOCK:]),
                jnp.where(head0, l[:BLOCK], l[BLOCK:]))

    def store(branch, rows, res):
        acc_ref[branch, rows, :] = res[0]
        m_ref[branch, rows, :] = res[1]
        l_ref[branch, rows, :] = res[2]

    unroll = ATTN_A_UNITS_PER_ITER
    nb4 = seq // 4 // BLOCK

    def body1(g, carry):
        for i in range(unroll):
            blk = g * unroll + i
            q0 = pl.multiple_of(blk * BLOCK, BLOCK)
            if i == 0:
                k0 = pl.multiple_of(jnp.maximum(blk - 1, 0) * BLOCK, BLOCK)
                bias = bias_ref[0, jnp.where(blk == 0, 1, 0)]
            else:
                k0 = pl.multiple_of((blk - 1) * BLOCK, BLOCK)
                bias = bias_ref[0, 0]
            store(0, pl.ds(q0, BLOCK),
                  unit(q1_ref[0, pl.ds(q0, BLOCK), :], k1_ref[0, pl.ds(k0, 2 * BLOCK), :],
                       v1_ref[0, pl.ds(k0, 2 * BLOCK), :], bias))
        return carry

    lax.fori_loop(0, seq // BLOCK // unroll, body1, 0)

    def body4(g, carry):
        for i in range(unroll):
            r = g * (unroll // nb4) + i // nb4
            blk = i % nb4
            k0 = max(blk - 1, 0) * BLOCK
            store(1, pl.ds(r + 4 * blk * BLOCK, BLOCK, stride=4),
                  unit(q4_ref[0, r, pl.ds(blk * BLOCK, BLOCK), :],
                       k4_ref[0, r, pl.ds(k0, 2 * BLOCK), :],
                       v4_ref[0, r, pl.ds(k0, 2 * BLOCK), :], bias_ref[1, 1 if blk == 0 else 0]))
        return carry

    lax.fori_loop(0, 4 * nb4 // unroll, body4, 0)

    def body16(g, carry):
        for i in range(unroll):
            r = g * unroll + i
            store(2, pl.ds(r, BLOCK, stride=16),
                  unit(q16_ref[0, r], k16_ref[0, r], v16_ref[0, r], bias16_ref[...]))
        return carry

    lax.fori_loop(0, 16 // unroll, body16, 0)

    def merge(blk, carry):
        rows = pl.ds(pl.multiple_of(blk * (2 * BLOCK), 2 * BLOCK), 2 * BLOCK)
        m0, m1, m2 = m_ref[0, rows, :], m_ref[1, rows, :], m_ref[2, rows, :]
        mx = jnp.maximum(jnp.maximum(m0, m1), m2)
        w0, w1, w2 = jnp.exp(m0 - mx), jnp.exp(m1 - mx), jnp.exp(m2 - mx)
        num = w0 * acc_ref[0, rows, :] + w1 * acc_ref[1, rows, :] + w2 * acc_ref[2, rows, :]
        den = w0 * l_ref[0, rows, :] + w1 * l_ref[1, rows, :] + w2 * l_ref[2, rows, :]
        o_ref[0, rows, :] = num / den
        return carry

    lax.fori_loop(0, seq // (2 * BLOCK), merge, 0)


def _attn_a(slopes, nat, a4, a16):
    _, bsz, seq, _ = nat.shape
    pairs = A_WIDTH // LANES

    def col(base):
        return lambda b, hp: (base + hp, b, 0, 0)

    def col4(base):
        return lambda b, hp: (b, base + hp, 0, 0, 0)

    nat_spec = [pl.BlockSpec((None, 1, seq, LANES), col(i * pairs)) for i in range(3)]
    a4_spec = [pl.BlockSpec((1, None, 4, seq // 4, LANES), col4(i * pairs)) for i in range(3)]
    a16_spec = [pl.BlockSpec((1, None, 16, seq // 16, LANES), col4(i * pairs)) for i in range(3)]
    return pl.pallas_call(
        _attn_a_kernel,
        grid=(bsz, pairs),
        in_specs=[pl.BlockSpec(memory_space=pltpu.SMEM)] + nat_spec + a4_spec + a16_spec,
        out_specs=pl.BlockSpec((1, seq, LANES), lambda b, hp: (b, 0, hp)),
        out_shape=jax.ShapeDtypeStruct((bsz, seq, A_WIDTH), F32),
        scratch_shapes=[pltpu.VMEM((3, seq, LANES), F32)] * 3 + [
            pltpu.VMEM((2, 2, 2 * BLOCK, 2 * BLOCK), F32), pltpu.VMEM((2 * BLOCK, BLOCK), F32)],
        compiler_params=_params("parallel", "parallel"),
        name="attn_a",
    )(slopes, nat, nat, nat, a4, a4, a4, a16, a16, a16)


def _attn_b_kernel(slopes_ref, sinks_ref, q_ref, k_ref, v_ref, o_ref, kd_ref, vd_ref, bias_ref):
    g = pl.program_id(1)
    seq = q_ref.shape[1]
    lane = lax.broadcasted_iota(jnp.int32, (1, LANES), 1)
    head0 = lane < HEAD_DIM
    mine = (lane >= HEAD_DIM).astype(jnp.int32) == g

    def both_halves(ref):
        x = ref[0].astype(F32)
        return jnp.where(mine, x, pltpu.roll(x, HEAD_DIM, 1)).astype(BF16)

    kd_ref[...] = both_halves(k_ref)
    vd_ref[...] = both_halves(v_ref)

    neg_slopes = [-slopes_ref[2 * (g * B_GROUP + j) + 1] for j in range(B_GROUP)]
    sinks = [sinks_ref[g * B_GROUP + j] for j in range(B_GROUP)]

    for lead in (False, True):
        qi = lax.broadcasted_iota(jnp.int32, (BLOCK, 2 * BLOCK), 0)
        kj = lax.broadcasted_iota(jnp.int32, (BLOCK, 2 * BLOCK), 1)
        rel = qi - kj if lead else qi + BLOCK - kj
        valid = (rel >= 0) & (rel <= B_WINDOW)
        rel_f = rel.astype(F32)
        for j in range(B_GROUP):
            bias_ref[int(lead), pl.ds(j * BLOCK, BLOCK), :] = jnp.where(valid, rel_f * neg_slopes[j], NEG_INF)

    def unit(blk, lead_possible):
        q0 = pl.multiple_of(blk * BLOCK, BLOCK)
        if lead_possible:
            k0 = pl.multiple_of(jnp.maximum(blk - 1, 0) * BLOCK, BLOCK)
            bias = bias_ref[jnp.where(blk == 0, 1, 0)]
        else:
            k0 = pl.multiple_of((blk - 1) * BLOCK, BLOCK)
            bias = bias_ref[0]
        k = kd_ref[pl.ds(k0, 2 * BLOCK), :]
        v = vd_ref[pl.ds(k0, 2 * BLOCK), :]
        q = q_ref[0, pl.ds(q0, BLOCK), :] * QK_SCALE
        stacked = []
        for j in range(B_GROUP):
            qp = q[:, (j // 2) * LANES:(j // 2 + 1) * LANES]
            stacked.append(jnp.where(head0 if j % 2 == 0 else ~head0, qp, jnp.zeros_like(qp)))
        s_all = _dot_nt(jnp.concatenate(stacked, axis=0), k) + bias
        probs, inv = [], []
        for j in range(B_GROUP):
            s = s_all[j * BLOCK:(j + 1) * BLOCK]
            m = jnp.maximum(jnp.max(s, axis=-1, keepdims=True), sinks[j])
            e = jnp.exp(s - m)
            inv.append(1.0 / (jnp.sum(e, axis=-1, keepdims=True) + jnp.exp(sinks[j] - m)))
            probs.append(e.astype(BF16))
        pv = _dot(jnp.concatenate(probs, axis=0), v)
        for p in range(B_GROUP // 2):
            even = pv[(2 * p) * BLOCK:(2 * p + 1) * BLOCK]
            odd = pv[(2 * p + 1) * BLOCK:(2 * p + 2) * BLOCK]
            o_ref[0, pl.ds(q0, BLOCK), p * LANES:(p + 1) * LANES] = (
                jnp.where(head0, even, odd) * jnp.where(head0, inv[2 * p], inv[2 * p + 1]))

    def body(it, carry):
        for i in range(ATTN_B_UNITS_PER_ITER):
            unit(it * ATTN_B_UNITS_PER_ITER + i, i == 0)
        return carry

    lax.fori_loop(0, seq // BLOCK // ATTN_B_UNITS_PER_ITER, body, 0)


def _attn_b(slopes, sinks, zb):
    bsz, seq, _ = zb.shape
    qw = B_GROUP * HEAD_DIM
    k_blk = B_WIDTH // LANES
    return pl.pallas_call(
        _attn_b_kernel,
        grid=(bsz, B_KV_HEADS),
        in_specs=[
            pl.BlockSpec(memory_space=pltpu.SMEM),
            pl.BlockSpec(memory_space=pltpu.SMEM),
            pl.BlockSpec((1, seq, qw), lambda b, g: (b, 0, g)),
            pl.BlockSpec((1, seq, LANES), lambda b, g: (b, 0, k_blk)),
            pl.BlockSpec((1, seq, LANES), lambda b, g: (b, 0, k_blk + 1)),
        ],
        out_specs=pl.BlockSpec((1, seq, qw), lambda b, g: (b, 0, g)),
        out_shape=jax.ShapeDtypeStruct((bsz, seq, B_WIDTH), F32),
        scratch_shapes=[pltpu.VMEM((seq, LANES), BF16)] * 2 + [
            pltpu.VMEM((2, B_GROUP * BLOCK, 2 * BLOCK), F32)],
        compiler_params=_params("parallel", "parallel"),
        name="attn_b",
    )(slopes, sinks, zb, zb, zb)


def _out_kernel(h_ref, oa_ref, ob_ref, ga_ref, gb_ref, wo_ref, bo_ref, gpost_ref, o_ref):
    na = _rms(oa_ref[...], ga_ref[...]).astype(BF16)
    nb = _rms(ob_ref[...], gb_ref[...]).astype(BF16)
    o = _dot(na, wo_ref[0:A_WIDTH, :]) + _dot(nb, wo_ref[A_WIDTH:A_WIDTH + B_WIDTH, :]) + bo_ref[...]
    o_ref[...] = h_ref[...] + _rms(o, gpost_ref[...])


def _out_proj(h, oa, ob, g_a, g_b, w_o, b_o, g_post):
    t, d = h.shape
    tm = OUT_TM
    row = lambda i: (i, 0)
    return pl.pallas_call(
        _out_kernel,
        grid=(t // tm,),
        in_specs=[
            pl.BlockSpec((tm, d), row),
            pl.BlockSpec((tm, A_WIDTH), row),
            pl.BlockSpec((tm, B_WIDTH), row),
            _resident((1, A_WIDTH)),
            _resident((1, B_WIDTH)),
            _resident((A_WIDTH + B_WIDTH, d)),
            _resident((1, d)),
            _resident((1, d)),
        ],
        out_specs=pl.BlockSpec((tm, d), row),
        out_shape=jax.ShapeDtypeStruct((t, d), F32),
        compiler_params=_params("parallel"),
        name="out_proj",
    )(h, oa, ob, g_a, g_b, w_o, b_o, g_post)


def _ple_kernel(h_ref, p_ref, gpre_ref, wg_ref, wp_ref, gpost_ref, o_ref):
    h = h_ref[...]
    gate = jax.nn.sigmoid(_dot(_rms(h, gpre_ref[...]).astype(BF16), wg_ref[...]))
    e = _dot(p_ref[...].astype(BF16), wp_ref[...])
    o_ref[...] = h + _rms(gate * e, gpost_ref[...])


def _ple(h, p, g_pre, w_gate, w_proj, g_post):
    t, d = h.shape
    pd = p.shape[1]
    tm = PLE_TM
    row = lambda i: (i, 0)
    return pl.pallas_call(
        _ple_kernel,
        grid=(t // tm,),
        in_specs=[
            pl.BlockSpec((tm, d), row),
            pl.BlockSpec((tm, pd), row),
            _resident((1, d)),
            _resident((d, d)),
            _resident((pd, d)),
            _resident((1, d)),
        ],
        out_specs=pl.BlockSpec((tm, d), row),
        out_shape=jax.ShapeDtypeStruct((t, d), F32),
        compiler_params=_params("parallel"),
        name="ple",
    )(h, p, g_pre, w_gate, w_proj, g_post)


def _alibi_slopes(n):
    return jnp.exp2(-ALIBI_MAX_BIAS * (jnp.arange(n, dtype=F32) + 1.0) / n)


def kernel(x, p, g_ffn1_pre, w_ffn1_gate, w_ffn1_up, w_ffn1_down, g_ffn1_post, g_mix_pre, w_qkv, b_qkv, attn_sinks, g_out_a, g_out_b, w_o, b_o, g_mix_post, g_ffn2_pre, w_ffn2_gate, w_ffn2_up, w_ffn2_down, g_ffn2_post, g_ple_pre, w_ple_gate, w_ple_proj, g_ple_post):
    bsz, seq, d = x.shape
    depth = p.shape[0]
    t = bsz * seq
    slopes = _alibi_slopes(A_HEADS + B_Q_HEADS)
    bf = lambda w: w.astype(BF16)
    h = x.reshape(t, d)
    for i in range(depth):
        h = _ffn(h, g_ffn1_pre[i][None], w_ffn1_gate[i], w_ffn1_up[i], w_ffn1_down[i], g_ffn1_post[i][None])
        nat, a4, a16, zb = _qkv(h, g_mix_pre[i][None], bf(w_qkv[i]), b_qkv[i][None], bsz, seq)
        oa = _attn_a(slopes, nat.reshape(-1, bsz, seq, LANES), a4, a16)
        ob = _attn_b(slopes, attn_sinks[i].reshape(-1), zb.reshape(bsz, seq, ZB_WIDTH))
        h = _out_proj(h, oa.reshape(t, A_WIDTH), ob.reshape(t, B_WIDTH), g_out_a[i][None], g_out_b[i][None],
                      bf(w_o[i]), b_o[i][None], g_mix_post[i][None])
        h = _ffn(h, g_ffn2_pre[i][None], w_ffn2_gate[i], w_ffn2_up[i], w_ffn2_down[i], g_ffn2_post[i][None])
        h = _ple(h, p[i].reshape(t, -1), g_ple_pre[i][None], bf(w_ple_gate[i]), bf(w_ple_proj[i]),
                 g_ple_post[i][None])
    return h.reshape(bsz, seq, d)
```

```python
import jax
import jax.numpy as jnp
from jax import lax
from jax.experimental import pallas as pl
from jax.experimental.pallas import tpu as pltpu

F32 = jnp.float32
BF16 = jnp.bfloat16

HEAD_DIM = 64
LANES = 128
MXU_WIDTH = 256
BLOCK = 128
A_HEADS = 16
B_Q_HEADS = 16
B_KV_HEADS = 2
B_GROUP = B_Q_HEADS // B_KV_HEADS
A_WIDTH = A_HEADS * HEAD_DIM
B_WIDTH = B_Q_HEADS * HEAD_DIM
KV_B_WIDTH = B_KV_HEADS * HEAD_DIM
ZB_WIDTH = B_WIDTH + 2 * KV_B_WIDTH
DILATIONS = (1, 4, 16)
A_WINDOW = 128
B_WINDOW = 127
ALIBI_MAX_BIAS = 8.0
EPS = 1e-6
QK_SCALE = HEAD_DIM ** -0.5
NEG_INF = float("-inf")

VMEM_LIMIT = 61 * 1024 * 1024

NORM_ROWS = 256
FFN_TM = 1024
FFN_TF = 512
FFN_SPLIT = 2
FFN_HEAD_TF = 256
QKV_TM = 512
ATTN_A_UNITS_PER_ITER = 16
ATTN_B_UNITS_PER_ITER = 8
OUT_TM = 512
PLE_TM = 1024


def _rms(x, g):
    return x * lax.rsqrt(jnp.mean(x * x, axis=-1, keepdims=True) + EPS) * g


def _dot(a, b):
    return jnp.dot(a, b, preferred_element_type=F32)


def _dot_nt(a, b):
    return lax.dot_general(a, b, (((1,), (1,)), ((), ())), preferred_element_type=F32)


def _resident(shape):
    return pl.BlockSpec(shape, lambda *_: (0,) * len(shape), pipeline_mode=pl.Buffered(1))


def _params(*semantics):
    return pltpu.CompilerParams(dimension_semantics=semantics, vmem_limit_bytes=VMEM_LIMIT)


def _ffn_steps(x_ref, gpre_ref, weights, gpost_ref, o_ref, xn_ref, active=True):
    j = pl.program_id(1)
    last = pl.num_programs(1) - 1
    tm = x_ref.shape[0]
    pieces = [pl.ds(i * (tm // FFN_SPLIT), tm // FFN_SPLIT) for i in range(FFN_SPLIT)]

    def prenorm(rows):
        for c in range(rows.size // NORM_ROWS):
            r = pl.ds(rows.start + c * NORM_ROWS, NORM_ROWS)
            xn_ref[r, :] = _rms(x_ref[r, :], gpre_ref[...]).astype(BF16)

    def postnorm(rows):
        for c in range(rows.size // NORM_ROWS):
            r = pl.ds(rows.start + c * NORM_ROWS, NORM_ROWS)
            o_ref[r, :] = x_ref[r, :] + 0.5 * _rms(o_ref[r, :], gpost_ref[...])

    def swiglu(w, rows, accumulate):
        wg, wu, wd = w
        xn = xn_ref[rows, :]
        gate = _dot(xn, wg)
        up = _dot(xn, wu)
        down = _dot((gate * jax.nn.sigmoid(gate) * up).astype(BF16), wd)
        o_ref[rows, :] = o_ref[rows, :] + down if accumulate else down

    @pl.when((j == 0) & active)
    def _():
        w = weights()
        for rows in pieces:
            prenorm(rows)
        for rows in pieces:
            swiglu(w, rows, False)

    @pl.when((j > 0) & (j < last) & active)
    def _():
        swiglu(weights(), pl.ds(0, tm), True)

    @pl.when((j == last) & active)
    def _():
        w = weights()
        for rows in pieces:
            swiglu(w, rows, True)
        for rows in pieces:
            postnorm(rows)


def _ffn_head_kernel(x_ref, gpre_ref, wg32_ref, wu32_ref, wd32_ref, gpost_ref,
                     o_ref, wg_ref, wu_ref, wd_ref, xn_ref):
    def weights():
        w = [r[...].astype(BF16) for r in (wg32_ref, wu32_ref, wd32_ref)]
        for dst, val in zip((wg_ref, wu_ref, wd_ref), w):
            dst[...] = val
        return w

    _ffn_steps(x_ref, gpre_ref, weights, gpost_ref, o_ref, xn_ref)


def _ffn_main_kernel(x_ref, gpre_ref, wg_ref, wu_ref, wd_ref, gpost_ref, o_ref, xn_ref):
    i = pl.program_id(0)

    @pl.when((i == 0) & (pl.program_id(1) == 0))
    def _():
        o_ref[...] = jnp.zeros_like(o_ref)

    _ffn_steps(x_ref, gpre_ref, lambda: (wg_ref[...], wu_ref[...], wd_ref[...]), gpost_ref, o_ref, xn_ref,
               active=i > 0)


def _ffn(h, g_pre, w_gate, w_up, w_down, g_post):
    t, d = h.shape
    d_ff = w_gate.shape[1]

    tm, tf = FFN_TM, FFN_HEAD_TF
    head_out, wg, wu, wd = pl.pallas_call(
        _ffn_head_kernel,
        grid=(1, d_ff // tf),
        in_specs=[
            pl.BlockSpec((tm, d), lambda i, j: (0, 0), pipeline_mode=pl.Buffered(1)),
            _resident((1, d)),
            pl.BlockSpec((d, tf), lambda i, j: (0, j)),
            pl.BlockSpec((d, tf), lambda i, j: (0, j)),
            pl.BlockSpec((tf, d), lambda i, j: (j, 0)),
            _resident((1, d)),
        ],
        out_specs=[
            pl.BlockSpec((tm, d), lambda i, j: (0, 0)),
            pl.BlockSpec((d, tf), lambda i, j: (0, j)),
            pl.BlockSpec((d, tf), lambda i, j: (0, j)),
            pl.BlockSpec((tf, d), lambda i, j: (j, 0)),
        ],
        out_shape=[jax.ShapeDtypeStruct((tm, d), F32), jax.ShapeDtypeStruct(w_gate.shape, BF16),
                   jax.ShapeDtypeStruct(w_up.shape, BF16), jax.ShapeDtypeStruct(w_down.shape, BF16)],
        scratch_shapes=[pltpu.VMEM((tm, d), BF16)],
        compiler_params=_params("arbitrary", "arbitrary"),
        name="ffn_head",
    )(h, g_pre, w_gate, w_up, w_down, g_post)

    tf = FFN_TF

    def step(i, j):
        return jnp.where(i == 0, 0, j)

    rest = pl.pallas_call(
        _ffn_main_kernel,
        grid=(t // tm, d_ff // tf),
        in_specs=[
            pl.BlockSpec((tm, d), lambda i, j: (i, 0)),
            _resident((1, d)),
            pl.BlockSpec((d, tf), lambda i, j: (0, step(i, j))),
            pl.BlockSpec((d, tf), lambda i, j: (0, step(i, j))),
            pl.BlockSpec((tf, d), lambda i, j: (step(i, j), 0)),
            _resident((1, d)),
        ],
        out_specs=pl.BlockSpec((tm, d), lambda i, j: (i, 0)),
        out_shape=jax.ShapeDtypeStruct((t, d), F32),
        scratch_shapes=[pltpu.VMEM((tm, d), BF16)],
        compiler_params=_params("parallel", "arbitrary"),
        name="ffn",
    )(h, g_pre, wg, wu, wd, g_post)
    return lax.dynamic_update_slice(rest, head_out, (0, 0))


def _qkv_kernel(h_ref, g_ref, w_ref, b_ref, nat_ref, a4_ref, a16_ref, zb_ref, zs_ref, z4_ref):
    tm = h_ref.shape[0]
    nc = zs_ref.shape[0]
    per_chunk = MXU_WIDTH // LANES
    xn = _rms(h_ref[...], g_ref[...]).astype(BF16)
    zb_ref[...] = (_dot(xn, w_ref[:, nc * LANES:]) + b_ref[:, nc * LANES:]).astype(BF16)
    for c2 in range(nc // per_chunk):
        wide = slice(c2 * MXU_WIDTH, (c2 + 1) * MXU_WIDTH)
        z2 = _dot(xn, w_ref[:, wide]) + b_ref[:, wide]
        for half in range(per_chunk):
            c = c2 * per_chunk + half
            zc = z2[:, half * LANES:(half + 1) * LANES]
            zs_ref[c] = zc
            nat_ref[c] = zc.astype(BF16)

    def copy(c, carry):
        for r in range(4):
            z4 = zs_ref[c, pl.ds(r, tm // 4, stride=4), :]
            a4_ref[0, c, r] = z4.astype(BF16)
            z4_ref[r] = z4
        for r in range(16):
            a16_ref[0, c, r] = z4_ref[r % 4, pl.ds(r // 4, tm // 16, stride=4), :].astype(BF16)
        return carry

    lax.fori_loop(0, nc, copy, 0)


def _qkv(h, g, w, b, bsz, seq):
    t, d = h.shape
    tm = QKV_TM
    nc = 3 * A_WIDTH // LANES
    per_seq = seq // tm
    return pl.pallas_call(
        _qkv_kernel,
        grid=(t // tm,),
        in_specs=[
            pl.BlockSpec((tm, d), lambda i: (i, 0)),
            _resident((1, d)),
            _resident(w.shape),
            _resident(b.shape),
        ],
        out_specs=[
            pl.BlockSpec((nc, tm, LANES), lambda i: (0, i, 0)),
            pl.BlockSpec((1, nc, 4, tm // 4, LANES), lambda i: (i // per_seq, 0, 0, i % per_seq, 0)),
            pl.BlockSpec((1, nc, 16, tm // 16, LANES), lambda i: (i // per_seq, 0, 0, i % per_seq, 0)),
            pl.BlockSpec((tm, ZB_WIDTH), lambda i: (i, 0)),
        ],
        out_shape=[
            jax.ShapeDtypeStruct((nc, t, LANES), BF16),
            jax.ShapeDtypeStruct((bsz, nc, 4, seq // 4, LANES), BF16),
            jax.ShapeDtypeStruct((bsz, nc, 16, seq // 16, LANES), BF16),
            jax.ShapeDtypeStruct((t, ZB_WIDTH), BF16),
        ],
        scratch_shapes=[pltpu.VMEM((nc, tm, LANES), F32), pltpu.VMEM((4, tm // 4, LANES), F32)],
        compiler_params=_params("parallel"),
        name="qkv",
    )(h, g, w, b)


def _attn_a_kernel(slopes_ref,
                   q1_ref, k1_ref, v1_ref, q4_ref, k4_ref, v4_ref, q16_ref, k16_ref, v16_ref,
                   o_ref, acc_ref, m_ref, l_ref, bias_ref, bias16_ref):
    hp = pl.program_id(1)
    seq = q1_ref.shape[1]
    lane = lax.broadcasted_iota(jnp.int32, (1, LANES), 1)
    head0 = lane < HEAD_DIM
    neg_slopes = (-slopes_ref[4 * hp], -slopes_ref[4 * hp + 2])

    def bias_table(nk, lead, dil):
        qi = lax.broadcasted_iota(jnp.int32, (BLOCK, nk), 0)
        kj = lax.broadcasted_iota(jnp.int32, (BLOCK, nk), 1)
        rel = qi - kj if lead else qi + (nk - BLOCK) - kj
        valid = (rel >= 0) & (rel <= A_WINDOW)
        rel_f = rel.astype(F32)
        return jnp.concatenate(
            [jnp.where(valid, rel_f * (ns * float(dil)), NEG_INF) for ns in neg_slopes], axis=0)

    for bi, dil in enumerate(DILATIONS[:2]):
        bias_ref[bi, 0] = bias_table(2 * BLOCK, False, dil)
        bias_ref[bi, 1] = bias_table(2 * BLOCK, True, dil)
    bias16_ref[...] = bias_table(BLOCK, True, DILATIONS[2])

    def unit(q, k, v, bias):
        zero = jnp.zeros_like(q)
        qs = jnp.concatenate([jnp.where(head0, q, zero), jnp.where(head0, zero, q)], axis=0)
        s = _dot_nt(qs * QK_SCALE, k) + bias
        m = jnp.max(s, axis=-1, keepdims=True)
        e = jnp.exp(s - m)
        l = jnp.sum(e, axis=-1, keepdims=True)
        pv = _dot(e.astype(BF16), v)
        return (jnp.where(head0, pv[:BLOCK], pv[BLOCK:]),
                jnp.where(head0, m[:BLOCK], m[BLOCK:]),
                jnp.where(head0, l[:BLOCK], l[BLOCK:]))

    def store(branch, rows, res):
        acc_ref[branch, rows, :] = res[0]
        m_ref[branch, rows, :] = res[1]
        l_ref[branch, rows, :] = res[2]

    unroll = ATTN_A_UNITS_PER_ITER
    nb4 = seq // 4 // BLOCK

    def body1(g, carry):
        for i in range(unroll):
            blk = g * unroll + i
            q0 = pl.multiple_of(blk * BLOCK, BLOCK)
            if i == 0:
                k0 = pl.multiple_of(jnp.maximum(blk - 1, 0) * BLOCK, BLOCK)
                bias = bias_ref[0, jnp.where(blk == 0, 1, 0)]
            else:
                k0 = pl.multiple_of((blk - 1) * BLOCK, BLOCK)
                bias = bias_ref[0, 0]
            store(0, pl.ds(q0, BLOCK),
                  unit(q1_ref[0, pl.ds(q0, BLOCK), :], k1_ref[0, pl.ds(k0, 2 * BLOCK), :],
                       v1_ref[0, pl.ds(k0, 2 * BLOCK), :], bias))
        return carry

    lax.fori_loop(0, seq // BLOCK // unroll, body1, 0)

    def body4(g, carry):
        for i in range(unroll):
            r = g * (unroll // nb4) + i // nb4
            blk = i % nb4
            k0 = max(blk - 1, 0) * BLOCK
            store(1, pl.ds(r + 4 * blk * BLOCK, BLOCK, stride=4),
                  unit(q4_ref[0, r, pl.ds(blk * BLOCK, BLOCK), :],
                       k4_ref[0, r, pl.ds(k0, 2 * BLOCK), :],
                       v4_ref[0, r, pl.ds(k0, 2 * BLOCK), :], bias_ref[1, 1 if blk == 0 else 0]))
        return carry

    lax.fori_loop(0, 4 * nb4 // unroll, body4, 0)

    def body16(g, carry):
        for i in range(unroll):
            r = g * unroll + i
            store(2, pl.ds(r, BLOCK, stride=16),
                  unit(q16_ref[0, r], k16_ref[0, r], v16_ref[0, r], bias16_ref[...]))
        return carry

    lax.fori_loop(0, 16 // unroll, body16, 0)

    def merge(blk, carry):
        rows = pl.ds(pl.multiple_of(blk * (2 * BLOCK), 2 * BLOCK), 2 * BLOCK)
        m0, m1, m2 = m_ref[0, rows, :], m_ref[1, rows, :], m_ref[2, rows, :]
        mx = jnp.maximum(jnp.maximum(m0, m1), m2)
        w0, w1, w2 = jnp.exp(m0 - mx), jnp.exp(m1 - mx), jnp.exp(m2 - mx)
        num = w0 * acc_ref[0, rows, :] + w1 * acc_ref[1, rows, :] + w2 * acc_ref[2, rows, :]
        den = w0 * l_ref[0, rows, :] + w1 * l_ref[1, rows, :] + w2 * l_ref[2, rows, :]
        o_ref[0, rows, :] = num / den
        return carry

    lax.fori_loop(0, seq // (2 * BLOCK), merge, 0)


def _attn_a(slopes, nat, a4, a16):
    _, bsz, seq, _ = nat.shape
    pairs = A_WIDTH // LANES

    def col(base):
        return lambda b, hp: (base + hp, b, 0, 0)

    def col4(base):
        return lambda b, hp: (b, base + hp, 0, 0, 0)

    nat_spec = [pl.BlockSpec((None, 1, seq, LANES), col(i * pairs)) for i in range(3)]
    a4_spec = [pl.BlockSpec((1, None, 4, seq // 4, LANES), col4(i * pairs)) for i in range(3)]
    a16_spec = [pl.BlockSpec((1, None, 16, seq // 16, LANES), col4(i * pairs)) for i in range(3)]
    return pl.pallas_call(
        _attn_a_kernel,
        grid=(bsz, pairs),
        in_specs=[pl.BlockSpec(memory_space=pltpu.SMEM)] + nat_spec + a4_spec + a16_spec,
        out_specs=pl.BlockSpec((1, seq, LANES), lambda b, hp: (b, 0, hp)),
        out_shape=jax.ShapeDtypeStruct((bsz, seq, A_WIDTH), F32),
        scratch_shapes=[pltpu.VMEM((3, seq, LANES), F32)] * 3 + [
            pltpu.VMEM((2, 2, 2 * BLOCK, 2 * BLOCK), F32), pltpu.VMEM((2 * BLOCK, BLOCK), F32)],
        compiler_params=_params("parallel", "parallel"),
        name="attn_a",
    )(slopes, nat, nat, nat, a4, a4, a4, a16, a16, a16)


def _attn_b_kernel(slopes_ref, sinks_ref, q_ref, k_ref, v_ref, o_ref, kd_ref, vd_ref, bias_ref):
    g = pl.program_id(1)
    seq = q_ref.shape[1]
    lane = lax.broadcasted_iota(jnp.int32, (1, LANES), 1)
    head0 = lane < HEAD_DIM
    mine = (lane >= HEAD_DIM).astype(jnp.int32) == g

    def both_halves(ref):
        x = ref[0].astype(F32)
        return jnp.where(mine, x, pltpu.roll(x, HEAD_DIM, 1)).astype(BF16)

    kd_ref[...] = both_halves(k_ref)
    vd_ref[...] = both_halves(v_ref)

    neg_slopes = [-slopes_ref[2 * (g * B_GROUP + j) + 1] for j in range(B_GROUP)]
    sinks = [sinks_ref[g * B_GROUP + j] for j in range(B_GROUP)]

    for lead in (False, True):
        qi = lax.broadcasted_iota(jnp.int32, (BLOCK, 2 * BLOCK), 0)
        kj = lax.broadcasted_iota(jnp.int32, (BLOCK, 2 * BLOCK), 1)
        rel = qi - kj if lead else qi + BLOCK - kj
        valid = (rel >= 0) & (rel <= B_WINDOW)
        rel_f = rel.astype(F32)
        for j in range(B_GROUP):
            bias_ref[int(lead), pl.ds(j * BLOCK, BLOCK), :] = jnp.where(valid, rel_f * neg_slopes[j], NEG_INF)

    def unit(blk, lead_possible):
        q0 = pl.multiple_of(blk * BLOCK, BLOCK)
        if lead_possible:
            k0 = pl.multiple_of(jnp.maximum(blk - 1, 0) * BLOCK, BLOCK)
            bias = bias_ref[jnp.where(blk == 0, 1, 0)]
        else:
            k0 = pl.multiple_of((blk - 1) * BLOCK, BLOCK)
            bias = bias_ref[0]
        k = kd_ref[pl.ds(k0, 2 * BLOCK), :]
        v = vd_ref[pl.ds(k0, 2 * BLOCK), :]
        q = q_ref[0, pl.ds(q0, BLOCK), :] * QK_SCALE
        stacked = []
        for j in range(B_GROUP):
            qp = q[:, (j // 2) * LANES:(j // 2 + 1) * LANES]
            stacked.append(jnp.where(head0 if j % 2 == 0 else ~head0, qp, jnp.zeros_like(qp)))
        s_all = _dot_nt(jnp.concatenate(stacked, axis=0), k) + bias
        probs, inv = [], []
        for j in range(B_GROUP):
            s = s_all[j * BLOCK:(j + 1) * BLOCK]
            m = jnp.maximum(jnp.max(s, axis=-1, keepdims=True), sinks[j])
            e = jnp.exp(s - m)
            inv.append(1.0 / (jnp.sum(e, axis=-1, keepdims=True) + jnp.exp(sinks[j] - m)))
            probs.append(e.astype(BF16))
        pv = _dot(jnp.concatenate(probs, axis=0), v)
        for p in range(B_GROUP // 2):
            even = pv[(2 * p) * BLOCK:(2 * p + 1) * BLOCK]
            odd = pv[(2 * p + 1) * BLOCK:(2 * p + 2) * BLOCK]
            o_ref[0, pl.ds(q0, BLOCK), p * LANES:(p + 1) * LANES] = (
                jnp.where(head0, even, odd) * jnp.where(head0, inv[2 * p], inv[2 * p + 1]))

    def body(it, carry):
        for i in range(ATTN_B_UNITS_PER_ITER):
            unit(it * ATTN_B_UNITS_PER_ITER + i, i == 0)
        return carry

    lax.fori_loop(0, seq // BLOCK // ATTN_B_UNITS_PER_ITER, body, 0)


def _attn_b(slopes, sinks, zb):
    bsz, seq, _ = zb.shape
    qw = B_GROUP * HEAD_DIM
    k_blk = B_WIDTH // LANES
    return pl.pallas_call(
        _attn_b_kernel,
        grid=(bsz, B_KV_HEADS),
        in_specs=[
            pl.BlockSpec(memory_space=pltpu.SMEM),
            pl.BlockSpec(memory_space=pltpu.SMEM),
            pl.BlockSpec((1, seq, qw), lambda b, g: (b, 0, g)),
            pl.BlockSpec((1, seq, LANES), lambda b, g: (b, 0, k_blk)),
            pl.BlockSpec((1, seq, LANES), lambda b, g: (b, 0, k_blk + 1)),
        ],
        out_specs=pl.BlockSpec((1, seq, qw), lambda b, g: (b, 0, g)),
        out_shape=jax.ShapeDtypeStruct((bsz, seq, B_WIDTH), F32),
        scratch_shapes=[pltpu.VMEM((seq, LANES), BF16)] * 2 + [
            pltpu.VMEM((2, B_GROUP * BLOCK, 2 * BLOCK), F32)],
        compiler_params=_params("parallel", "parallel"),
        name="attn_b",
    )(slopes, sinks, zb, zb, zb)


def _out_kernel(h_ref, oa_ref, ob_ref, ga_ref, gb_ref, wo32_ref, bo_ref, gpost_ref, o_ref, wo_ref):
    @pl.when(pl.program_id(0) == 0)
    def _():
        wo_ref[...] = wo32_ref[...].astype(BF16)

    na = _rms(oa_ref[...], ga_ref[...]).astype(BF16)
    nb = _rms(ob_ref[...], gb_ref[...]).astype(BF16)
    o = _dot(na, wo_ref[0:A_WIDTH, :]) + _dot(nb, wo_ref[A_WIDTH:A_WIDTH + B_WIDTH, :]) + bo_ref[...]
    o_ref[...] = h_ref[...] + _rms(o, gpost_ref[...])


def _out_proj(h, oa, ob, g_a, g_b, w_o, b_o, g_post):
    t, d = h.shape
    tm = OUT_TM
    row = lambda i: (i, 0)
    return pl.pallas_call(
        _out_kernel,
        grid=(t // tm,),
        in_specs=[
            pl.BlockSpec((tm, d), row),
            pl.BlockSpec((tm, A_WIDTH), row),
            pl.BlockSpec((tm, B_WIDTH), row),
            _resident((1, A_WIDTH)),
            _resident((1, B_WIDTH)),
            _resident((A_WIDTH + B_WIDTH, d)),
            _resident((1, d)),
            _resident((1, d)),
        ],
        out_specs=pl.BlockSpec((tm, d), row),
        out_shape=jax.ShapeDtypeStruct((t, d), F32),
        scratch_shapes=[pltpu.VMEM(w_o.shape, BF16)],
        compiler_params=_params("arbitrary"),
        name="out_proj",
    )(h, oa, ob, g_a, g_b, w_o, b_o, g_post)


def _ple_kernel(h_ref, p_ref, gpre_ref, wg_ref, wp_ref, gpost_ref, o_ref):
    h = h_ref[...]
    gate = jax.nn.sigmoid(_dot(_rms(h, gpre_ref[...]).astype(BF16), wg_ref[...]))
    e = _dot(p_ref[...].astype(BF16), wp_ref[...])
    o_ref[...] = h + _rms(gate * e, gpost_ref[...])


def _ple(h, p, g_pre, w_gate, w_proj, g_post):
    t, d = h.shape
    pd = p.shape[1]
    tm = PLE_TM
    row = lambda i: (i, 0)
    return pl.pallas_call(
        _ple_kernel,
        grid=(t // tm,),
        in_specs=[
            pl.BlockSpec((tm, d), row),
            pl.BlockSpec((tm, pd), row),
            _resident((1, d)),
            _resident((d, d)),
            _resident((pd, d)),
            _resident((1, d)),
        ],
        out_specs=pl.BlockSpec((tm, d), row),
        out_shape=jax.ShapeDtypeStruct((t, d), F32),
        compiler_params=_params("parallel"),
        name="ple",
    )(h, p, g_pre, w_gate, w_proj, g_post)


def _alibi_slopes(n):
    return jnp.exp2(-ALIBI_MAX_BIAS * (jnp.arange(n, dtype=F32) + 1.0) / n)


def kernel(x, p, g_ffn1_pre, w_ffn1_gate, w_ffn1_up, w_ffn1_down, g_ffn1_post, g_mix_pre, w_qkv, b_qkv, attn_sinks, g_out_a, g_out_b, w_o, b_o, g_mix_post, g_ffn2_pre, w_ffn2_gate, w_ffn2_up, w_ffn2_down, g_ffn2_post, g_ple_pre, w_ple_gate, w_ple_proj, g_ple_post):
    bsz, seq, d = x.shape
    depth = p.shape[0]
    t = bsz * seq
    slopes = _alibi_slopes(A_HEADS + B_Q_HEADS)
    bf = lambda w: w.astype(BF16)
    h = x.reshape(t, d)
    for i in range(depth):
        h = _ffn(h, g_ffn1_pre[i][None], w_ffn1_gate[i], w_ffn1_up[i], w_ffn1_down[i], g_ffn1_post[i][None])
        nat, a4, a16, zb = _qkv(h, g_mix_pre[i][None], bf(w_qkv[i]), b_qkv[i][None], bsz, seq)
        oa = _attn_a(slopes, nat.reshape(-1, bsz, seq, LANES), a4, a16)
        ob = _attn_b(slopes, attn_sinks[i].reshape(-1), zb.reshape(bsz, seq, ZB_WIDTH))
        h = _out_proj(h, oa.reshape(t, A_WIDTH), ob.reshape(t, B_WIDTH), g_out_a[i][None], g_out_b[i][None],
                      w_o[i], b_o[i][None], g_mix_post[i][None])
        h = _ffn(h, g_ffn2_pre[i][None], w_ffn2_gate[i], w_ffn2_up[i], w_ffn2_down[i], g_ffn2_post[i][None])
        h = _ple(h, p[i].reshape(t, -1), g_ple_pre[i][None], bf(w_ple_gate[i]), bf(w_ple_proj[i]),
                 g_ple_post[i][None])
    return h.reshape(bsz, seq, d)
```

```python
import jax
import jax.numpy as jnp
from jax import lax
from jax.experimental import pallas as pl
from jax.experimental.pallas import tpu as pltpu

F32 = jnp.float32
BF16 = jnp.bfloat16

HEAD_DIM = 64
LANES = 128
MXU_WIDTH = 256
BLOCK = 128
A_HEADS = 16
B_Q_HEADS = 16
B_KV_HEADS = 2
B_GROUP = B_Q_HEADS // B_KV_HEADS
A_WIDTH = A_HEADS * HEAD_DIM
B_WIDTH = B_Q_HEADS * HEAD_DIM
KV_B_WIDTH = B_KV_HEADS * HEAD_DIM
ZB_WIDTH = B_WIDTH + 2 * KV_B_WIDTH
DILATIONS = (1, 4, 16)
A_WINDOW = 128
B_WINDOW = 127
ALIBI_MAX_BIAS = 8.0
EPS = 1e-6
QK_SCALE = HEAD_DIM ** -0.5
NEG_INF = float("-inf")

VMEM_LIMIT = 61 * 1024 * 1024

NORM_ROWS = 256
FFN_TM = 1024
FFN_TF = 512
FFN_SPLIT = 2
FFN_HEAD_TF = 256
QKV_TM = 512
ATTN_A_UNITS_PER_ITER = 16
ATTN_B_UNITS_PER_ITER = 8
OUT_TM = 512
PLE_TM = 1024


def _rms(x, g):
    return x * lax.rsqrt(jnp.mean(x * x, axis=-1, keepdims=True) + EPS) * g


def _dot(a, b):
    return jnp.dot(a, b, preferred_element_type=F32)


def _dot_nt(a, b):
    return lax.dot_general(a, b, (((1,), (1,)), ((), ())), preferred_element_type=F32)


def _resident(shape):
    return pl.BlockSpec(shape, lambda *_: (0,) * len(shape), pipeline_mode=pl.Buffered(1))


def _params(*semantics):
    return pltpu.CompilerParams(dimension_semantics=semantics, vmem_limit_bytes=VMEM_LIMIT)


def _ffn_steps(x_ref, gpre_ref, weights, gpost_ref, o_ref, xn_ref, active=True):
    j = pl.program_id(1)
    last = pl.num_programs(1) - 1
    tm = x_ref.shape[0]
    pieces = [pl.ds(i * (tm // FFN_SPLIT), tm // FFN_SPLIT) for i in range(FFN_SPLIT)]

    def prenorm(rows):
        for c in range(rows.size // NORM_ROWS):
            r = pl.ds(rows.start + c * NORM_ROWS, NORM_ROWS)
            xn_ref[r, :] = _rms(x_ref[r, :], gpre_ref[...]).astype(BF16)

    def postnorm(rows):
        for c in range(rows.size // NORM_ROWS):
            r = pl.ds(rows.start + c * NORM_ROWS, NORM_ROWS)
            o_ref[r, :] = x_ref[r, :] + 0.5 * _rms(o_ref[r, :], gpost_ref[...])

    def swiglu(w, rows, accumulate):
        wg, wu, wd = w
        xn = xn_ref[rows, :]
        gate = _dot(xn, wg)
        up = _dot(xn, wu)
        down = _dot((gate * jax.nn.sigmoid(gate) * up).astype(BF16), wd)
        o_ref[rows, :] = o_ref[rows, :] + down if accumulate else down

    @pl.when((j == 0) & active)
    def _():
        w = weights()
        for rows in pieces:
            prenorm(rows)
        for rows in pieces:
            swiglu(w, rows, False)

    @pl.when((j > 0) & (j < last) & active)
    def _():
        swiglu(weights(), pl.ds(0, tm), True)

    @pl.when((j == last) & active)
    def _():
        w = weights()
        for rows in pieces:
            swiglu(w, rows, True)
        for rows in pieces:
            postnorm(rows)


def _ffn_head_kernel(x_ref, gpre_ref, wg32_ref, wu32_ref, wd32_ref, gpost_ref,
                     o_ref, wg_ref, wu_ref, wd_ref, xn_ref):
    def weights():
        w = [r[...].astype(BF16) for r in (wg32_ref, wu32_ref, wd32_ref)]
        for dst, val in zip((wg_ref, wu_ref, wd_ref), w):
            dst[...] = val
        return w

    _ffn_steps(x_ref, gpre_ref, weights, gpost_ref, o_ref, xn_ref)


def _ffn_main_kernel(x_ref, gpre_ref, wg_ref, wu_ref, wd_ref, gpost_ref, o_ref, xn_ref):
    i = pl.program_id(0)

    @pl.when((i == 0) & (pl.program_id(1) == 0))
    def _():
        o_ref[...] = jnp.zeros_like(o_ref)

    _ffn_steps(x_ref, gpre_ref, lambda: (wg_ref[...], wu_ref[...], wd_ref[...]), gpost_ref, o_ref, xn_ref,
               active=i > 0)


def _ffn(h, g_pre, w_gate, w_up, w_down, g_post):
    t, d = h.shape
    d_ff = w_gate.shape[1]

    tm, tf = FFN_TM, FFN_HEAD_TF
    head_out, wg, wu, wd = pl.pallas_call(
        _ffn_head_kernel,
        grid=(1, d_ff // tf),
        in_specs=[
            pl.BlockSpec((tm, d), lambda i, j: (0, 0), pipeline_mode=pl.Buffered(1)),
            _resident((1, d)),
            pl.BlockSpec((d, tf), lambda i, j: (0, j)),
            pl.BlockSpec((d, tf), lambda i, j: (0, j)),
            pl.BlockSpec((tf, d), lambda i, j: (j, 0)),
            _resident((1, d)),
        ],
        out_specs=[
            pl.BlockSpec((tm, d), lambda i, j: (0, 0)),
            pl.BlockSpec((d, tf), lambda i, j: (0, j)),
            pl.BlockSpec((d, tf), lambda i, j: (0, j)),
            pl.BlockSpec((tf, d), lambda i, j: (j, 0)),
        ],
        out_shape=[jax.ShapeDtypeStruct((tm, d), F32), jax.ShapeDtypeStruct(w_gate.shape, BF16),
                   jax.ShapeDtypeStruct(w_up.shape, BF16), jax.ShapeDtypeStruct(w_down.shape, BF16)],
        scratch_shapes=[pltpu.VMEM((tm, d), BF16)],
        compiler_params=_params("arbitrary", "arbitrary"),
        name="ffn_head",
    )(h, g_pre, w_gate, w_up, w_down, g_post)

    tf = FFN_TF

    def step(i, j):
        return jnp.where(i == 0, 0, j)

    rest = pl.pallas_call(
        _ffn_main_kernel,
        grid=(t // tm, d_ff // tf),
        in_specs=[
            pl.BlockSpec((tm, d), lambda i, j: (i, 0)),
            _resident((1, d)),
            pl.BlockSpec((d, tf), lambda i, j: (0, step(i, j))),
            pl.BlockSpec((d, tf), lambda i, j: (0, step(i, j))),
            pl.BlockSpec((tf, d), lambda i, j: (step(i, j), 0)),
            _resident((1, d)),
        ],
        out_specs=pl.BlockSpec((tm, d), lambda i, j: (i, 0)),
        out_shape=jax.ShapeDtypeStruct((t, d), F32),
        scratch_shapes=[pltpu.VMEM((tm, d), BF16)],
        compiler_params=_params("parallel", "arbitrary"),
        name="ffn",
    )(h, g_pre, wg, wu, wd, g_post)
    return lax.dynamic_update_slice(rest, head_out, (0, 0))


def _qkv_kernel(h_ref, g_ref, w_ref, b_ref, nat_ref, a4_ref, a16_ref, zb_ref, zs_ref, z4_ref):
    tm = h_ref.shape[0]
    nc = zs_ref.shape[0]
    per_chunk = MXU_WIDTH // LANES
    xn = _rms(h_ref[...], g_ref[...]).astype(BF16)
    zb_ref[...] = (_dot(xn, w_ref[:, nc * LANES:]) + b_ref[:, nc * LANES:]).astype(BF16)
    for c2 in range(nc // per_chunk):
        wide = slice(c2 * MXU_WIDTH, (c2 + 1) * MXU_WIDTH)
        z2 = _dot(xn, w_ref[:, wide]) + b_ref[:, wide]
        for half in range(per_chunk):
            c = c2 * per_chunk + half
            zc = z2[:, half * LANES:(half + 1) * LANES]
            zs_ref[c] = zc
            nat_ref[c] = zc.astype(BF16)

    def copy(c, carry):
        for r in range(4):
            z4 = zs_ref[c, pl.ds(r, tm // 4, stride=4), :]
            a4_ref[0, c, r] = z4.astype(BF16)
            z4_ref[r] = z4
        for r in range(16):
            a16_ref[0, c, r] = z4_ref[r % 4, pl.ds(r // 4, tm // 16, stride=4), :].astype(BF16)
        return carry

    lax.fori_loop(0, nc, copy, 0)


def _qkv(h, g, w, b, bsz, seq):
    t, d = h.shape
    tm = QKV_TM
    nc = 3 * A_WIDTH // LANES
    per_seq = seq // tm
    return pl.pallas_call(
        _qkv_kernel,
        grid=(t // tm,),
        in_specs=[
            pl.BlockSpec((tm, d), lambda i: (i, 0)),
            _resident((1, d)),
            _resident(w.shape),
            _resident(b.shape),
        ],
        out_specs=[
            pl.BlockSpec((nc, tm, LANES), lambda i: (0, i, 0)),
            pl.BlockSpec((1, nc, 4, tm // 4, LANES), lambda i: (i // per_seq, 0, 0, i % per_seq, 0)),
            pl.BlockSpec((1, nc, 16, tm // 16, LANES), lambda i: (i // per_seq, 0, 0, i % per_seq, 0)),
            pl.BlockSpec((tm, ZB_WIDTH), lambda i: (i, 0)),
        ],
        out_shape=[
            jax.ShapeDtypeStruct((nc, t, LANES), BF16),
            jax.ShapeDtypeStruct((bsz, nc, 4, seq // 4, LANES), BF16),
            jax.ShapeDtypeStruct((bsz, nc, 16, seq // 16, LANES), BF16),
            jax.ShapeDtypeStruct((t, ZB_WIDTH), BF16),
        ],
        scratch_shapes=[pltpu.VMEM((nc, tm, LANES), F32), pltpu.VMEM((4, tm // 4, LANES), F32)],
        compiler_params=_params("parallel"),
        name="qkv",
    )(h, g, w, b)


def _attn_a_kernel(slopes_ref,
                   q1_ref, k1_ref, v1_ref, q4_ref, k4_ref, v4_ref, q16_ref, k16_ref, v16_ref,
                   o_ref, acc_ref, m_ref, l_ref, bias_ref, bias16_ref):
    hp = pl.program_id(1)
    seq = q1_ref.shape[1]
    lane = lax.broadcasted_iota(jnp.int32, (1, LANES), 1)
    head0 = lane < HEAD_DIM
    neg_slopes = (-slopes_ref[4 * hp], -slopes_ref[4 * hp + 2])

    def bias_table(nk, lead, dil):
        qi = lax.broadcasted_iota(jnp.int32, (BLOCK, nk), 0)
        kj = lax.broadcasted_iota(jnp.int32, (BLOCK, nk), 1)
        rel = qi - kj if lead else qi + (nk - BLOCK) - kj
        valid = (rel >= 0) & (rel <= A_WINDOW)
        rel_f = rel.astype(F32)
        return jnp.concatenate(
            [jnp.where(valid, rel_f * (ns * float(dil)), NEG_INF) for ns in neg_slopes], axis=0)

    @pl.when(pl.program_id(0) == 0)
    def _():
        for bi, dil in enumerate(DILATIONS[:2]):
            bias_ref[hp, bi, 0] = bias_table(2 * BLOCK, False, dil)
            bias_ref[hp, bi, 1] = bias_table(2 * BLOCK, True, dil)
        bias16_ref[hp] = bias_table(BLOCK, True, DILATIONS[2])

    def unit(q, k, v, bias):
        zero = jnp.zeros_like(q)
        qs = jnp.concatenate([jnp.where(head0, q, zero), jnp.where(head0, zero, q)], axis=0)
        s = _dot_nt(qs * QK_SCALE, k) + bias
        m = jnp.max(s, axis=-1, keepdims=True)
        e = jnp.exp(s - m)
        l = jnp.sum(e, axis=-1, keepdims=True)
        pv = _dot(e.astype(BF16), v)
        return (jnp.where(head0, pv[:BLOCK], pv[BLOCK:]),
                jnp.where(head0, m[:BLOCK], m[BLOCK:]),
                jnp.where(head0, l[:BLOCK], l[BLOCK:]))

    def store(branch, rows, res):
        acc_ref[branch, rows, :] = res[0]
        m_ref[branch, rows, :] = res[1]
        l_ref[branch, rows, :] = res[2]

    unroll = ATTN_A_UNITS_PER_ITER
    nb4 = seq // 4 // BLOCK

    def body1(g, carry):
        for i in range(unroll):
            blk = g * unroll + i
            q0 = pl.multiple_of(blk * BLOCK, BLOCK)
            if i == 0:
                k0 = pl.multiple_of(jnp.maximum(blk - 1, 0) * BLOCK, BLOCK)
                bias = bias_ref[hp, 0, jnp.where(blk == 0, 1, 0)]
            else:
                k0 = pl.multiple_of((blk - 1) * BLOCK, BLOCK)
                bias = bias_ref[hp, 0, 0]
            store(0, pl.ds(q0, BLOCK),
                  unit(q1_ref[0, pl.ds(q0, BLOCK), :], k1_ref[0, pl.ds(k0, 2 * BLOCK), :],
                       v1_ref[0, pl.ds(k0, 2 * BLOCK), :], bias))
        return carry

    lax.fori_loop(0, seq // BLOCK // unroll, body1, 0)

    def body4(g, carry):
        for i in range(unroll):
            r = g * (unroll // nb4) + i // nb4
            blk = i % nb4
            k0 = max(blk - 1, 0) * BLOCK
            store(1, pl.ds(r + 4 * blk * BLOCK, BLOCK, stride=4),
                  unit(q4_ref[0, r, pl.ds(blk * BLOCK, BLOCK), :],
                       k4_ref[0, r, pl.ds(k0, 2 * BLOCK), :],
                       v4_ref[0, r, pl.ds(k0, 2 * BLOCK), :], bias_ref[hp, 1, 1 if blk == 0 else 0]))
        return carry

    lax.fori_loop(0, 4 * nb4 // unroll, body4, 0)

    def body16(g, carry):
        for i in range(unroll):
            r = g * unroll + i
            store(2, pl.ds(r, BLOCK, stride=16),
                  unit(q16_ref[0, r], k16_ref[0, r], v16_ref[0, r], bias16_ref[hp]))
        return carry

    lax.fori_loop(0, 16 // unroll, body16, 0)

    def merge(blk, carry):
        rows = pl.ds(pl.multiple_of(blk * (2 * BLOCK), 2 * BLOCK), 2 * BLOCK)
        m0, m1, m2 = m_ref[0, rows, :], m_ref[1, rows, :], m_ref[2, rows, :]
        mx = jnp.maximum(jnp.maximum(m0, m1), m2)
        w0, w1, w2 = jnp.exp(m0 - mx), jnp.exp(m1 - mx), jnp.exp(m2 - mx)
        num = w0 * acc_ref[0, rows, :] + w1 * acc_ref[1, rows, :] + w2 * acc_ref[2, rows, :]
        den = w0 * l_ref[0, rows, :] + w1 * l_ref[1, rows, :] + w2 * l_ref[2, rows, :]
        o_ref[0, rows, :] = num / den
        return carry

    lax.fori_loop(0, seq // (2 * BLOCK), merge, 0)


def _attn_a(slopes, nat, a4, a16):
    _, bsz, seq, _ = nat.shape
    pairs = A_WIDTH // LANES

    def col(base):
        return lambda b, hp: (base + hp, b, 0, 0)

    def col4(base):
        return lambda b, hp: (b, base + hp, 0, 0, 0)

    nat_spec = [pl.BlockSpec((None, 1, seq, LANES), col(i * pairs)) for i in range(3)]
    a4_spec = [pl.BlockSpec((1, None, 4, seq // 4, LANES), col4(i * pairs)) for i in range(3)]
    a16_spec = [pl.BlockSpec((1, None, 16, seq // 16, LANES), col4(i * pairs)) for i in range(3)]
    return pl.pallas_call(
        _attn_a_kernel,
        grid=(bsz, pairs),
        in_specs=[pl.BlockSpec(memory_space=pltpu.SMEM)] + nat_spec + a4_spec + a16_spec,
        out_specs=pl.BlockSpec((1, seq, LANES), lambda b, hp: (b, 0, hp)),
        out_shape=jax.ShapeDtypeStruct((bsz, seq, A_WIDTH), F32),
        scratch_shapes=[pltpu.VMEM((3, seq, LANES), F32)] * 3 + [
            pltpu.VMEM((pairs, 2, 2, 2 * BLOCK, 2 * BLOCK), F32), pltpu.VMEM((pairs, 2 * BLOCK, BLOCK), F32)],
        compiler_params=_params("arbitrary", "arbitrary"),
        name="attn_a",
    )(slopes, nat, nat, nat, a4, a4, a4, a16, a16, a16)


def _attn_b_kernel(slopes_ref, sinks_ref, q_ref, k_ref, v_ref, o_ref, kd_ref, vd_ref, bias_ref):
    g = pl.program_id(1)
    seq = q_ref.shape[1]
    lane = lax.broadcasted_iota(jnp.int32, (1, LANES), 1)
    head0 = lane < HEAD_DIM
    mine = (lane >= HEAD_DIM).astype(jnp.int32) == g

    def both_halves(ref):
        x = ref[0].astype(F32)
        return jnp.where(mine, x, pltpu.roll(x, HEAD_DIM, 1)).astype(BF16)

    kd_ref[...] = both_halves(k_ref)
    vd_ref[...] = both_halves(v_ref)

    neg_slopes = [-slopes_ref[2 * (g * B_GROUP + j) + 1] for j in range(B_GROUP)]
    sinks = [sinks_ref[g * B_GROUP + j] for j in range(B_GROUP)]

    @pl.when(pl.program_id(0) == 0)
    def _():
        for lead in (False, True):
            qi = lax.broadcasted_iota(jnp.int32, (BLOCK, 2 * BLOCK), 0)
            kj = lax.broadcasted_iota(jnp.int32, (BLOCK, 2 * BLOCK), 1)
            rel = qi - kj if lead else qi + BLOCK - kj
            valid = (rel >= 0) & (rel <= B_WINDOW)
            rel_f = rel.astype(F32)
            for j in range(B_GROUP):
                bias_ref[g, int(lead), pl.ds(j * BLOCK, BLOCK), :] = (
                    jnp.where(valid, rel_f * neg_slopes[j], NEG_INF))

    def unit(blk, lead_possible):
        q0 = pl.multiple_of(blk * BLOCK, BLOCK)
        if lead_possible:
            k0 = pl.multiple_of(jnp.maximum(blk - 1, 0) * BLOCK, BLOCK)
            bias = bias_ref[g, jnp.where(blk == 0, 1, 0)]
        else:
            k0 = pl.multiple_of((blk - 1) * BLOCK, BLOCK)
            bias = bias_ref[g, 0]
        k = kd_ref[pl.ds(k0, 2 * BLOCK), :]
        v = vd_ref[pl.ds(k0, 2 * BLOCK), :]
        q = q_ref[0, pl.ds(q0, BLOCK), :] * QK_SCALE
        stacked = []
        for j in range(B_GROUP):
            qp = q[:, (j // 2) * LANES:(j // 2 + 1) * LANES]
            stacked.append(jnp.where(head0 if j % 2 == 0 else ~head0, qp, jnp.zeros_like(qp)))
        s_all = _dot_nt(jnp.concatenate(stacked, axis=0), k) + bias
        probs, inv = [], []
        for j in range(B_GROUP):
            s = s_all[j * BLOCK:(j + 1) * BLOCK]
            m = jnp.maximum(jnp.max(s, axis=-1, keepdims=True), sinks[j])
            e = jnp.exp(s - m)
            inv.append(1.0 / (jnp.sum(e, axis=-1, keepdims=True) + jnp.exp(sinks[j] - m)))
            probs.append(e.astype(BF16))
        pv = _dot(jnp.concatenate(probs, axis=0), v)
        for p in range(B_GROUP // 2):
            even = pv[(2 * p) * BLOCK:(2 * p + 1) * BLOCK]
            odd = pv[(2 * p + 1) * BLOCK:(2 * p + 2) * BLOCK]
            o_ref[0, pl.ds(q0, BLOCK), p * LANES:(p + 1) * LANES] = (
                jnp.where(head0, even, odd) * jnp.where(head0, inv[2 * p], inv[2 * p + 1]))

    def body(it, carry):
        for i in range(ATTN_B_UNITS_PER_ITER):
            unit(it * ATTN_B_UNITS_PER_ITER + i, i == 0)
        return carry

    lax.fori_loop(0, seq // BLOCK // ATTN_B_UNITS_PER_ITER, body, 0)


def _attn_b(slopes, sinks, zb):
    bsz, seq, _ = zb.shape
    qw = B_GROUP * HEAD_DIM
    k_blk = B_WIDTH // LANES
    return pl.pallas_call(
        _attn_b_kernel,
        grid=(bsz, B_KV_HEADS),
        in_specs=[
            pl.BlockSpec(memory_space=pltpu.SMEM),
            pl.BlockSpec(memory_space=pltpu.SMEM),
            pl.BlockSpec((1, seq, qw), lambda b, g: (b, 0, g)),
            pl.BlockSpec((1, seq, LANES), lambda b, g: (b, 0, k_blk)),
            pl.BlockSpec((1, seq, LANES), lambda b, g: (b, 0, k_blk + 1)),
        ],
        out_specs=pl.BlockSpec((1, seq, qw), lambda b, g: (b, 0, g)),
        out_shape=jax.ShapeDtypeStruct((bsz, seq, B_WIDTH), F32),
        scratch_shapes=[pltpu.VMEM((seq, LANES), BF16)] * 2 + [
            pltpu.VMEM((B_KV_HEADS, 2, B_GROUP * BLOCK, 2 * BLOCK), F32)],
        compiler_params=_params("arbitrary", "arbitrary"),
        name="attn_b",
    )(slopes, sinks, zb, zb, zb)


def _out_kernel(h_ref, oa_ref, ob_ref, ga_ref, gb_ref, wo32_ref, bo_ref, gpost_ref, o_ref, wo_ref):
    @pl.when(pl.program_id(0) == 0)
    def _():
        wo_ref[...] = wo32_ref[...].astype(BF16)

    na = _rms(oa_ref[...], ga_ref[...]).astype(BF16)
    nb = _rms(ob_ref[...], gb_ref[...]).astype(BF16)
    o = _dot(na, wo_ref[0:A_WIDTH, :]) + _dot(nb, wo_ref[A_WIDTH:A_WIDTH + B_WIDTH, :]) + bo_ref[...]
    o_ref[...] = h_ref[...] + _rms(o, gpost_ref[...])


def _out_proj(h, oa, ob, g_a, g_b, w_o, b_o, g_post):
    t, d = h.shape
    tm = OUT_TM
    row = lambda i: (i, 0)
    return pl.pallas_call(
        _out_kernel,
        grid=(t // tm,),
        in_specs=[
            pl.BlockSpec((tm, d), row),
            pl.BlockSpec((tm, A_WIDTH), row),
            pl.BlockSpec((tm, B_WIDTH), row),
            _resident((1, A_WIDTH)),
            _resident((1, B_WIDTH)),
            _resident((A_WIDTH + B_WIDTH, d)),
            _resident((1, d)),
            _resident((1, d)),
        ],
        out_specs=pl.BlockSpec((tm, d), row),
        out_shape=jax.ShapeDtypeStruct((t, d), F32),
        scratch_shapes=[pltpu.VMEM(w_o.shape, BF16)],
        compiler_params=_params("arbitrary"),
        name="out_proj",
    )(h, oa, ob, g_a, g_b, w_o, b_o, g_post)


def _ple_kernel(h_ref, p_ref, gpre_ref, wg_ref, wp_ref, gpost_ref, o_ref):
    h = h_ref[...]
    gate = jax.nn.sigmoid(_dot(_rms(h, gpre_ref[...]).astype(BF16), wg_ref[...]))
    e = _dot(p_ref[...].astype(BF16), wp_ref[...])
    o_ref[...] = h + _rms(gate * e, gpost_ref[...])


def _ple(h, p, g_pre, w_gate, w_proj, g_post):
    t, d = h.shape
    pd = p.shape[1]
    tm = PLE_TM
    row = lambda i: (i, 0)
    return pl.pallas_call(
        _ple_kernel,
        grid=(t // tm,),
        in_specs=[
            pl.BlockSpec((tm, d), row),
            pl.BlockSpec((tm, pd), row),
            _resident((1, d)),
            _resident((d, d)),
            _resident((pd, d)),
            _resident((1, d)),
        ],
        out_specs=pl.BlockSpec((tm, d), row),
        out_shape=jax.ShapeDtypeStruct((t, d), F32),
        compiler_params=_params("parallel"),
        name="ple",
    )(h, p, g_pre, w_gate, w_proj, g_post)


def _alibi_slopes(n):
    return jnp.exp2(-ALIBI_MAX_BIAS * (jnp.arange(n, dtype=F32) + 1.0) / n)


def kernel(x, p, g_ffn1_pre, w_ffn1_gate, w_ffn1_up, w_ffn1_down, g_ffn1_post, g_mix_pre, w_qkv, b_qkv, attn_sinks, g_out_a, g_out_b, w_o, b_o, g_mix_post, g_ffn2_pre, w_ffn2_gate, w_ffn2_up, w_ffn2_down, g_ffn2_post, g_ple_pre, w_ple_gate, w_ple_proj, g_ple_post):
    bsz, seq, d = x.shape
    depth = p.shape[0]
    t = bsz * seq
    slopes = _alibi_slopes(A_HEADS + B_Q_HEADS)
    bf = lambda w: w.astype(BF16)
    h = x.reshape(t, d)
    for i in range(depth):
        h = _ffn(h, g_ffn1_pre[i][None], w_ffn1_gate[i], w_ffn1_up[i], w_ffn1_down[i], g_ffn1_post[i][None])
        nat, a4, a16, zb = _qkv(h, g_mix_pre[i][None], bf(w_qkv[i]), b_qkv[i][None], bsz, seq)
        oa = _attn_a(slopes, nat.reshape(-1, bsz, seq, LANES), a4, a16)
        ob = _attn_b(slopes, attn_sinks[i].reshape(-1), zb.reshape(bsz, seq, ZB_WIDTH))
        h = _out_proj(h, oa.reshape(t, A_WIDTH), ob.reshape(t, B_WIDTH), g_out_a[i][None], g_out_b[i][None],
                      w_o[i], b_o[i][None], g_mix_post[i][None])
        h = _ffn(h, g_ffn2_pre[i][None], w_ffn2_gate[i], w_ffn2_up[i], w_ffn2_down[i], g_ffn2_post[i][None])
        h = _ple(h, p[i].reshape(t, -1), g_ple_pre[i][None], bf(w_ple_gate[i]), bf(w_ple_proj[i]),
                 g_ple_post[i][None])
    return h.reshape(bsz, seq, d)
```

```python
import jax
import jax.numpy as jnp
from jax import lax
from jax.experimental import pallas as pl
from jax.experimental.pallas import tpu as pltpu

F32 = jnp.float32
BF16 = jnp.bfloat16

HEAD_DIM = 64
LANES = 128
MXU_WIDTH = 256
BLOCK = 128
A_HEADS = 16
B_Q_HEADS = 16
B_KV_HEADS = 2
B_GROUP = B_Q_HEADS // B_KV_HEADS
A_WIDTH = A_HEADS * HEAD_DIM
B_WIDTH = B_Q_HEADS * HEAD_DIM
KV_B_WIDTH = B_KV_HEADS * HEAD_DIM
ZB_WIDTH = B_WIDTH + 2 * KV_B_WIDTH
DILATIONS = (1, 4, 16)
A_WINDOW = 128
B_WINDOW = 127
ALIBI_MAX_BIAS = 8.0
EPS = 1e-6
QK_SCALE = HEAD_DIM ** -0.5
NEG_INF = float("-inf")

VMEM_LIMIT = 61 * 1024 * 1024

NORM_ROWS = 256
FFN_TM = 1024
FFN_TF = 512
FFN_SPLIT = 2
FFN_HEAD_TF = 256
QKV_TM = 512
ATTN_A_UNITS_PER_ITER = 16
ATTN_B_UNITS_PER_ITER = 8
OUT_TM = 512
PLE_TM = 1024


def _rms(x, g):
    return x * lax.rsqrt(jnp.mean(x * x, axis=-1, keepdims=True) + EPS) * g


def _dot(a, b):
    return jnp.dot(a, b, preferred_element_type=F32)


def _dot_nt(a, b):
    return lax.dot_general(a, b, (((1,), (1,)), ((), ())), preferred_element_type=F32)


def _resident(shape):
    return pl.BlockSpec(shape, lambda *_: (0,) * len(shape), pipeline_mode=pl.Buffered(1))


def _params(*semantics):
    return pltpu.CompilerParams(dimension_semantics=semantics, vmem_limit_bytes=VMEM_LIMIT)


def _ffn_steps(x_ref, gpre_ref, weights, gpost_ref, o_ref, xn_ref, active=True):
    j = pl.program_id(1)
    last = pl.num_programs(1) - 1
    tm = x_ref.shape[0]
    pieces = [pl.ds(i * (tm // FFN_SPLIT), tm // FFN_SPLIT) for i in range(FFN_SPLIT)]

    def prenorm(rows):
        for c in range(rows.size // NORM_ROWS):
            r = pl.ds(rows.start + c * NORM_ROWS, NORM_ROWS)
            xn_ref[r, :] = _rms(x_ref[r, :], gpre_ref[...]).astype(BF16)

    def postnorm(rows):
        for c in range(rows.size // NORM_ROWS):
            r = pl.ds(rows.start + c * NORM_ROWS, NORM_ROWS)
            o_ref[r, :] = x_ref[r, :] + 0.5 * _rms(o_ref[r, :], gpost_ref[...])

    def swiglu(w, rows, accumulate):
        wg, wu, wd = w
        xn = xn_ref[rows, :]
        gate = _dot(xn, wg)
        up = _dot(xn, wu)
        down = _dot((gate * jax.nn.sigmoid(gate) * up).astype(BF16), wd)
        o_ref[rows, :] = o_ref[rows, :] + down if accumulate else down

    @pl.when((j == 0) & active)
    def _():
        w = weights()
        for rows in pieces:
            prenorm(rows)
        for rows in pieces:
            swiglu(w, rows, False)

    @pl.when((j > 0) & (j < last) & active)
    def _():
        swiglu(weights(), pl.ds(0, tm), True)

    @pl.when((j == last) & active)
    def _():
        w = weights()
        for rows in pieces:
            swiglu(w, rows, True)
        for rows in pieces:
            postnorm(rows)


def _ffn_head_kernel(x_ref, gpre_ref, wg32_ref, wu32_ref, wd32_ref, gpost_ref,
                     o_ref, wg_ref, wu_ref, wd_ref, xn_ref):
    def weights():
        w = [r[...].astype(BF16) for r in (wg32_ref, wu32_ref, wd32_ref)]
        for dst, val in zip((wg_ref, wu_ref, wd_ref), w):
            dst[...] = val
        return w

    _ffn_steps(x_ref, gpre_ref, weights, gpost_ref, o_ref, xn_ref)


def _ffn_main_kernel(x_ref, gpre_ref, wg_ref, wu_ref, wd_ref, gpost_ref, o_ref, xn_ref):
    i = pl.program_id(0)

    @pl.when((i == 0) & (pl.program_id(1) == 0))
    def _():
        o_ref[...] = jnp.zeros_like(o_ref)

    _ffn_steps(x_ref, gpre_ref, lambda: (wg_ref[...], wu_ref[...], wd_ref[...]), gpost_ref, o_ref, xn_ref,
               active=i > 0)


def _ffn(h, g_pre, w_gate, w_up, w_down, g_post):
    t, d = h.shape
    d_ff = w_gate.shape[1]

    tm, tf = FFN_TM, FFN_HEAD_TF
    head_out, wg, wu, wd = pl.pallas_call(
        _ffn_head_kernel,
        grid=(1, d_ff // tf),
        in_specs=[
            pl.BlockSpec((tm, d), lambda i, j: (0, 0), pipeline_mode=pl.Buffered(1)),
            _resident((1, d)),
            pl.BlockSpec((d, tf), lambda i, j: (0, j)),
            pl.BlockSpec((d, tf), lambda i, j: (0, j)),
            pl.BlockSpec((tf, d), lambda i, j: (j, 0)),
            _resident((1, d)),
        ],
        out_specs=[
            pl.BlockSpec((tm, d), lambda i, j: (0, 0)),
            pl.BlockSpec((d, tf), lambda i, j: (0, j)),
            pl.BlockSpec((d, tf), lambda i, j: (0, j)),
            pl.BlockSpec((tf, d), lambda i, j: (j, 0)),
        ],
        out_shape=[jax.ShapeDtypeStruct((tm, d), F32), jax.ShapeDtypeStruct(w_gate.shape, BF16),
                   jax.ShapeDtypeStruct(w_up.shape, BF16), jax.ShapeDtypeStruct(w_down.shape, BF16)],
        scratch_shapes=[pltpu.VMEM((tm, d), BF16)],
        compiler_params=_params("arbitrary", "arbitrary"),
        name="ffn_head",
    )(h, g_pre, w_gate, w_up, w_down, g_post)

    tf = FFN_TF

    def step(i, j):
        return jnp.where(i == 0, 0, j)

    rest = pl.pallas_call(
        _ffn_main_kernel,
        grid=(t // tm, d_ff // tf),
        in_specs=[
            pl.BlockSpec((tm, d), lambda i, j: (i, 0)),
            _resident((1, d)),
            pl.BlockSpec((d, tf), lambda i, j: (0, step(i, j))),
            pl.BlockSpec((d, tf), lambda i, j: (0, step(i, j))),
            pl.BlockSpec((tf, d), lambda i, j: (step(i, j), 0)),
            _resident((1, d)),
        ],
        out_specs=pl.BlockSpec((tm, d), lambda i, j: (i, 0)),
        out_shape=jax.ShapeDtypeStruct((t, d), F32),
        scratch_shapes=[pltpu.VMEM((tm, d), BF16)],
        compiler_params=_params("parallel", "arbitrary"),
        name="ffn",
    )(h, g_pre, wg, wu, wd, g_post)
    return lax.dynamic_update_slice(rest, head_out, (0, 0))


def _qkv_kernel(h_ref, g_ref, w_ref, b_ref, nat_ref, a4_ref, a16_ref, zb_ref, zs_ref, z4_ref):
    tm = h_ref.shape[0]
    nc = zs_ref.shape[0]
    per_chunk = MXU_WIDTH // LANES
    xn = _rms(h_ref[...], g_ref[...]).astype(BF16)
    zb_ref[...] = (_dot(xn, w_ref[:, nc * LANES:]) + b_ref[:, nc * LANES:]).astype(BF16)
    for c2 in range(nc // per_chunk):
        wide = slice(c2 * MXU_WIDTH, (c2 + 1) * MXU_WIDTH)
        z2 = _dot(xn, w_ref[:, wide]) + b_ref[:, wide]
        for half in range(per_chunk):
            c = c2 * per_chunk + half
            zc = z2[:, half * LANES:(half + 1) * LANES]
            zs_ref[c] = zc
            nat_ref[c] = zc.astype(BF16)

    def copy(c, carry):
        for r in range(4):
            z4 = zs_ref[c, pl.ds(r, tm // 4, stride=4), :]
            a4_ref[0, c, r] = z4.astype(BF16)
            z4_ref[r] = z4
        for r in range(16):
            a16_ref[0, c, r] = z4_ref[r % 4, pl.ds(r // 4, tm // 16, stride=4), :].astype(BF16)
        return carry

    lax.fori_loop(0, nc, copy, 0)


def _qkv(h, g, w, b, bsz, seq):
    t, d = h.shape
    tm = QKV_TM
    nc = 3 * A_WIDTH // LANES
    per_seq = seq // tm
    return pl.pallas_call(
        _qkv_kernel,
        grid=(t // tm,),
        in_specs=[
            pl.BlockSpec((tm, d), lambda i: (i, 0)),
            _resident((1, d)),
            _resident(w.shape),
            _resident(b.shape),
        ],
        out_specs=[
            pl.BlockSpec((nc, tm, LANES), lambda i: (0, i, 0)),
            pl.BlockSpec((1, nc, 4, tm // 4, LANES), lambda i: (i // per_seq, 0, 0, i % per_seq, 0)),
            pl.BlockSpec((1, nc, 16, tm // 16, LANES), lambda i: (i // per_seq, 0, 0, i % per_seq, 0)),
            pl.BlockSpec((tm, ZB_WIDTH), lambda i: (i, 0)),
        ],
        out_shape=[
            jax.ShapeDtypeStruct((nc, t, LANES), BF16),
            jax.ShapeDtypeStruct((bsz, nc, 4, seq // 4, LANES), BF16),
            jax.ShapeDtypeStruct((bsz, nc, 16, seq // 16, LANES), BF16),
            jax.ShapeDtypeStruct((t, ZB_WIDTH), BF16),
        ],
        scratch_shapes=[pltpu.VMEM((nc, tm, LANES), F32), pltpu.VMEM((4, tm // 4, LANES), F32)],
        compiler_params=_params("parallel"),
        name="qkv",
    )(h, g, w, b)


def _attn_a_kernel(slopes_ref,
                   q1_ref, k1_ref, v1_ref, q4_ref, k4_ref, v4_ref, q16_ref, k16_ref, v16_ref,
                   o_ref, acc_ref, m_ref, l_ref, bias_ref, bias16_ref):
    hp = pl.program_id(1)
    seq = q1_ref.shape[1]
    lane = lax.broadcasted_iota(jnp.int32, (1, LANES), 1)
    head0 = lane < HEAD_DIM
    neg_slopes = (-slopes_ref[4 * hp], -slopes_ref[4 * hp + 2])

    def bias_table(nk, lead, dil):
        qi = lax.broadcasted_iota(jnp.int32, (BLOCK, nk), 0)
        kj = lax.broadcasted_iota(jnp.int32, (BLOCK, nk), 1)
        rel = qi - kj if lead else qi + (nk - BLOCK) - kj
        valid = (rel >= 0) & (rel <= A_WINDOW)
        rel_f = rel.astype(F32)
        return jnp.concatenate(
            [jnp.where(valid, rel_f * (ns * float(dil)), NEG_INF) for ns in neg_slopes], axis=0)

    for bi, dil in enumerate(DILATIONS[:2]):
        bias_ref[bi, 0] = bias_table(2 * BLOCK, False, dil)
        bias_ref[bi, 1] = bias_table(2 * BLOCK, True, dil)
    bias16_ref[...] = bias_table(BLOCK, True, DILATIONS[2])

    def unit(q, k, v, bias):
        zero = jnp.zeros_like(q)
        qs = jnp.concatenate([jnp.where(head0, q, zero), jnp.where(head0, zero, q)], axis=0)
        s = _dot_nt(qs * QK_SCALE, k) + bias
        m = jnp.max(s, axis=-1, keepdims=True)
        e = jnp.exp(s - m)
        l = jnp.sum(e, axis=-1, keepdims=True)
        pv = _dot(e.astype(BF16), v)
        return (jnp.where(head0, pv[:BLOCK], pv[BLOCK:]),
                jnp.where(head0, m[:BLOCK], m[BLOCK:]),
                jnp.where(head0, l[:BLOCK], l[BLOCK:]))

    def store(branch, rows, res):
        acc_ref[branch, rows, :] = res[0]
        m_ref[branch, rows, :] = res[1]
        l_ref[branch, rows, :] = res[2]

    unroll = ATTN_A_UNITS_PER_ITER
    nb4 = seq // 4 // BLOCK

    def body1(g, carry):
        for i in range(unroll):
            blk = g * unroll + i
            q0 = pl.multiple_of(blk * BLOCK, BLOCK)
            if i == 0:
                k0 = pl.multiple_of(jnp.maximum(blk - 1, 0) * BLOCK, BLOCK)
                bias = bias_ref[0, jnp.where(blk == 0, 1, 0)]
            else:
                k0 = pl.multiple_of((blk - 1) * BLOCK, BLOCK)
                bias = bias_ref[0, 0]
            store(0, pl.ds(q0, BLOCK),
                  unit(q1_ref[0, pl.ds(q0, BLOCK), :], k1_ref[0, pl.ds(k0, 2 * BLOCK), :],
                       v1_ref[0, pl.ds(k0, 2 * BLOCK), :], bias))
        return carry

    lax.fori_loop(0, seq // BLOCK // unroll, body1, 0)

    def body4(g, carry):
        for i in range(unroll):
            r = g * (unroll // nb4) + i // nb4
            blk = i % nb4
            k0 = max(blk - 1, 0) * BLOCK
            store(1, pl.ds(r + 4 * blk * BLOCK, BLOCK, stride=4),
                  unit(q4_ref[0, r, pl.ds(blk * BLOCK, BLOCK), :],
                       k4_ref[0, r, pl.ds(k0, 2 * BLOCK), :],
                       v4_ref[0, r, pl.ds(k0, 2 * BLOCK), :], bias_ref[1, 1 if blk == 0 else 0]))
        return carry

    lax.fori_loop(0, 4 * nb4 // unroll, body4, 0)

    def body16(g, carry):
        for i in range(unroll):
            r = g * unroll + i
            store(2, pl.ds(r, BLOCK, stride=16),
                  unit(q16_ref[0, r], k16_ref[0, r], v16_ref[0, r], bias16_ref[...]))
        return carry

    lax.fori_loop(0, 16 // unroll, body16, 0)

    def merge(blk, carry):
        rows = pl.ds(pl.multiple_of(blk * (2 * BLOCK), 2 * BLOCK), 2 * BLOCK)
        m0, m1, m2 = m_ref[0, rows, :], m_ref[1, rows, :], m_ref[2, rows, :]
        mx = jnp.maximum(jnp.maximum(m0, m1), m2)
        w0, w1, w2 = jnp.exp(m0 - mx), jnp.exp(m1 - mx), jnp.exp(m2 - mx)
        num = w0 * acc_ref[0, rows, :] + w1 * acc_ref[1, rows, :] + w2 * acc_ref[2, rows, :]
        den = w0 * l_ref[0, rows, :] + w1 * l_ref[1, rows, :] + w2 * l_ref[2, rows, :]
        o_ref[0, rows, :] = num / den
        return carry

    lax.fori_loop(0, seq // (2 * BLOCK), merge, 0)


def _attn_a(slopes, nat, a4, a16):
    _, bsz, seq, _ = nat.shape
    pairs = A_WIDTH // LANES

    def col(base):
        return lambda b, hp: (base + hp, b, 0, 0)

    def col4(base):
        return lambda b, hp: (b, base + hp, 0, 0, 0)

    nat_spec = [pl.BlockSpec((None, 1, seq, LANES), col(i * pairs)) for i in range(3)]
    a4_spec = [pl.BlockSpec((1, None, 4, seq // 4, LANES), col4(i * pairs)) for i in range(3)]
    a16_spec = [pl.BlockSpec((1, None, 16, seq // 16, LANES), col4(i * pairs)) for i in range(3)]
    return pl.pallas_call(
        _attn_a_kernel,
        grid=(bsz, pairs),
        in_specs=[pl.BlockSpec(memory_space=pltpu.SMEM)] + nat_spec + a4_spec + a16_spec,
        out_specs=pl.BlockSpec((None, 1, seq, LANES), lambda b, hp: (hp, b, 0, 0)),
        out_shape=jax.ShapeDtypeStruct((pairs, bsz, seq, LANES), F32),
        scratch_shapes=[pltpu.VMEM((3, seq, LANES), F32)] * 3 + [
            pltpu.VMEM((2, 2, 2 * BLOCK, 2 * BLOCK), F32), pltpu.VMEM((2 * BLOCK, BLOCK), F32)],
        compiler_params=_params("parallel", "parallel"),
        name="attn_a",
    )(slopes, nat, nat, nat, a4, a4, a4, a16, a16, a16)


def _attn_b_kernel(slopes_ref, sinks_ref, q_ref, k_ref, v_ref, o_ref, kd_ref, vd_ref, bias_ref):
    g = pl.program_id(1)
    seq = q_ref.shape[1]
    lane = lax.broadcasted_iota(jnp.int32, (1, LANES), 1)
    head0 = lane < HEAD_DIM
    mine = (lane >= HEAD_DIM).astype(jnp.int32) == g

    def both_halves(ref):
        x = ref[0].astype(F32)
        return jnp.where(mine, x, pltpu.roll(x, HEAD_DIM, 1)).astype(BF16)

    kd_ref[...] = both_halves(k_ref)
    vd_ref[...] = both_halves(v_ref)

    neg_slopes = [-slopes_ref[2 * (g * B_GROUP + j) + 1] for j in range(B_GROUP)]
    sinks = [sinks_ref[g * B_GROUP + j] for j in range(B_GROUP)]

    for lead in (False, True):
        qi = lax.broadcasted_iota(jnp.int32, (BLOCK, 2 * BLOCK), 0)
        kj = lax.broadcasted_iota(jnp.int32, (BLOCK, 2 * BLOCK), 1)
        rel = qi - kj if lead else qi + BLOCK - kj
        valid = (rel >= 0) & (rel <= B_WINDOW)
        rel_f = rel.astype(F32)
        for j in range(B_GROUP):
            bias_ref[int(lead), pl.ds(j * BLOCK, BLOCK), :] = jnp.where(valid, rel_f * neg_slopes[j], NEG_INF)

    def unit(blk, lead_possible):
        q0 = pl.multiple_of(blk * BLOCK, BLOCK)
        if lead_possible:
            k0 = pl.multiple_of(jnp.maximum(blk - 1, 0) * BLOCK, BLOCK)
            bias = bias_ref[jnp.where(blk == 0, 1, 0)]
        else:
            k0 = pl.multiple_of((blk - 1) * BLOCK, BLOCK)
            bias = bias_ref[0]
        k = kd_ref[pl.ds(k0, 2 * BLOCK), :]
        v = vd_ref[pl.ds(k0, 2 * BLOCK), :]
        q = q_ref[0, pl.ds(q0, BLOCK), :] * QK_SCALE
        stacked = []
        for j in range(B_GROUP):
            qp = q[:, (j // 2) * LANES:(j // 2 + 1) * LANES]
            stacked.append(jnp.where(head0 if j % 2 == 0 else ~head0, qp, jnp.zeros_like(qp)))
        s_all = _dot_nt(jnp.concatenate(stacked, axis=0), k) + bias
        probs, inv = [], []
        for j in range(B_GROUP):
            s = s_all[j * BLOCK:(j + 1) * BLOCK]
            m = jnp.maximum(jnp.max(s, axis=-1, keepdims=True), sinks[j])
            e = jnp.exp(s - m)
            inv.append(1.0 / (jnp.sum(e, axis=-1, keepdims=True) + jnp.exp(sinks[j] - m)))
            probs.append(e.astype(BF16))
        pv = _dot(jnp.concatenate(probs, axis=0), v)
        for p in range(B_GROUP // 2):
            even = pv[(2 * p) * BLOCK:(2 * p + 1) * BLOCK]
            odd = pv[(2 * p + 1) * BLOCK:(2 * p + 2) * BLOCK]
            o_ref[0, pl.ds(q0, BLOCK), p * LANES:(p + 1) * LANES] = (
                jnp.where(head0, even, odd) * jnp.where(head0, inv[2 * p], inv[2 * p + 1]))

    def body(it, carry):
        for i in range(ATTN_B_UNITS_PER_ITER):
            unit(it * ATTN_B_UNITS_PER_ITER + i, i == 0)
        return carry

    lax.fori_loop(0, seq // BLOCK // ATTN_B_UNITS_PER_ITER, body, 0)


def _attn_b(slopes, sinks, zb):
    bsz, seq, _ = zb.shape
    qw = B_GROUP * HEAD_DIM
    k_blk = B_WIDTH // LANES
    return pl.pallas_call(
        _attn_b_kernel,
        grid=(bsz, B_KV_HEADS),
        in_specs=[
            pl.BlockSpec(memory_space=pltpu.SMEM),
            pl.BlockSpec(memory_space=pltpu.SMEM),
            pl.BlockSpec((1, seq, qw), lambda b, g: (b, 0, g)),
            pl.BlockSpec((1, seq, LANES), lambda b, g: (b, 0, k_blk)),
            pl.BlockSpec((1, seq, LANES), lambda b, g: (b, 0, k_blk + 1)),
        ],
        out_specs=pl.BlockSpec((1, seq, qw), lambda b, g: (b, 0, g)),
        out_shape=jax.ShapeDtypeStruct((bsz, seq, B_WIDTH), F32),
        scratch_shapes=[pltpu.VMEM((seq, LANES), BF16)] * 2 + [
            pltpu.VMEM((2, B_GROUP * BLOCK, 2 * BLOCK), F32)],
        compiler_params=_params("parallel", "parallel"),
        name="attn_b",
    )(slopes, sinks, zb, zb, zb)


def _out_kernel(h_ref, oa_ref, ob_ref, ga_ref, gb_ref, wo32_ref, bo_ref, gpost_ref, o_ref, wo_ref):
    @pl.when(pl.program_id(0) == 0)
    def _():
        wo_ref[...] = wo32_ref[...].astype(BF16)

    oa = jnp.concatenate([oa_ref[c] for c in range(oa_ref.shape[0])], axis=1)
    na = _rms(oa, ga_ref[...]).astype(BF16)
    nb = _rms(ob_ref[...], gb_ref[...]).astype(BF16)
    o = _dot(na, wo_ref[0:A_WIDTH, :]) + _dot(nb, wo_ref[A_WIDTH:A_WIDTH + B_WIDTH, :]) + bo_ref[...]
    o_ref[...] = h_ref[...] + _rms(o, gpost_ref[...])


def _out_proj(h, oa, ob, g_a, g_b, w_o, b_o, g_post):
    t, d = h.shape
    tm = OUT_TM
    row = lambda i: (i, 0)
    return pl.pallas_call(
        _out_kernel,
        grid=(t // tm,),
        in_specs=[
            pl.BlockSpec((tm, d), row),
            pl.BlockSpec((A_WIDTH // LANES, tm, LANES), lambda i: (0, i, 0)),
            pl.BlockSpec((tm, B_WIDTH), row),
            _resident((1, A_WIDTH)),
            _resident((1, B_WIDTH)),
            _resident((A_WIDTH + B_WIDTH, d)),
            _resident((1, d)),
            _resident((1, d)),
        ],
        out_specs=pl.BlockSpec((tm, d), row),
        out_shape=jax.ShapeDtypeStruct((t, d), F32),
        scratch_shapes=[pltpu.VMEM(w_o.shape, BF16)],
        compiler_params=_params("arbitrary"),
        name="out_proj",
    )(h, oa, ob, g_a, g_b, w_o, b_o, g_post)


def _ple_kernel(h_ref, p_ref, gpre_ref, wg_ref, wp_ref, gpost_ref, o_ref):
    h = h_ref[...]
    gate = jax.nn.sigmoid(_dot(_rms(h, gpre_ref[...]).astype(BF16), wg_ref[...]))
    e = _dot(p_ref[...].astype(BF16), wp_ref[...])
    o_ref[...] = h + _rms(gate * e, gpost_ref[...])


def _ple(h, p, g_pre, w_gate, w_proj, g_post):
    t, d = h.shape
    pd = p.shape[1]
    tm = PLE_TM
    row = lambda i: (i, 0)
    return pl.pallas_call(
        _ple_kernel,
        grid=(t // tm,),
        in_specs=[
            pl.BlockSpec((tm, d), row),
            pl.BlockSpec((tm, pd), row),
            _resident((1, d)),
            _resident((d, d)),
            _resident((pd, d)),
            _resident((1, d)),
        ],
        out_specs=pl.BlockSpec((tm, d), row),
        out_shape=jax.ShapeDtypeStruct((t, d), F32),
        compiler_params=_params("parallel"),
        name="ple",
    )(h, p, g_pre, w_gate, w_proj, g_post)


def _alibi_slopes(n):
    return jnp.exp2(-ALIBI_MAX_BIAS * (jnp.arange(n, dtype=F32) + 1.0) / n)


def kernel(x, p, g_ffn1_pre, w_ffn1_gate, w_ffn1_up, w_ffn1_down, g_ffn1_post, g_mix_pre, w_qkv, b_qkv, attn_sinks, g_out_a, g_out_b, w_o, b_o, g_mix_post, g_ffn2_pre, w_ffn2_gate, w_ffn2_up, w_ffn2_down, g_ffn2_post, g_ple_pre, w_ple_gate, w_ple_proj, g_ple_post):
    bsz, seq, d = x.shape
    depth = p.shape[0]
    t = bsz * seq
    slopes = _alibi_slopes(A_HEADS + B_Q_HEADS)
    bf = lambda w: w.astype(BF16)
    h = x.reshape(t, d)
    for i in range(depth):
        h = _ffn(h, g_ffn1_pre[i][None], w_ffn1_gate[i], w_ffn1_up[i], w_ffn1_down[i], g_ffn1_post[i][None])
        nat, a4, a16, zb = _qkv(h, g_mix_pre[i][None], bf(w_qkv[i]), b_qkv[i][None], bsz, seq)
        oa = _attn_a(slopes, nat.reshape(-1, bsz, seq, LANES), a4, a16)
        ob = _attn_b(slopes, attn_sinks[i].reshape(-1), zb.reshape(bsz, seq, ZB_WIDTH))
        h = _out_proj(h, oa.reshape(-1, t, LANES), ob.reshape(t, B_WIDTH), g_out_a[i][None], g_out_b[i][None],
                      w_o[i], b_o[i][None], g_mix_post[i][None])
        h = _ffn(h, g_ffn2_pre[i][None], w_ffn2_gate[i], w_ffn2_up[i], w_ffn2_down[i], g_ffn2_post[i][None])
        h = _ple(h, p[i].reshape(t, -1), g_ple_pre[i][None], bf(w_ple_gate[i]), bf(w_ple_proj[i]),
                 g_ple_post[i][None])
    return h.reshape(bsz, seq, d)
```

```python
import jax
import jax.numpy as jnp
from jax import lax
from jax.experimental import pallas as pl
from jax.experimental.pallas import tpu as pltpu

F32 = jnp.float32
BF16 = jnp.bfloat16

HEAD_DIM = 64
LANES = 128
MXU_WIDTH = 256
BLOCK = 128
A_HEADS = 16
B_Q_HEADS = 16
B_KV_HEADS = 2
B_GROUP = B_Q_HEADS // B_KV_HEADS
A_WIDTH = A_HEADS * HEAD_DIM
B_WIDTH = B_Q_HEADS * HEAD_DIM
KV_B_WIDTH = B_KV_HEADS * HEAD_DIM
ZB_WIDTH = B_WIDTH + 2 * KV_B_WIDTH
DILATIONS = (1, 4, 16)
A_WINDOW = 128
B_WINDOW = 127
ALIBI_MAX_BIAS = 8.0
EPS = 1e-6
QK_SCALE = HEAD_DIM ** -0.5
NEG_INF = float("-inf")

VMEM_LIMIT = 61 * 1024 * 1024

NORM_ROWS = 256
FFN_TM = 1024
FFN_TF = 512
FFN_SPLIT = 2
FFN_HEAD_TF = 256
FFN_HEAD_BUFFERS = 3
QKV_TM = 512
ATTN_A_UNITS_PER_ITER = 16
ATTN_B_UNITS_PER_ITER = 8
OUT_TM = 512
PLE_TM = 1024


def _rms(x, g):
    return x * lax.rsqrt(jnp.mean(x * x, axis=-1, keepdims=True) + EPS) * g


def _dot(a, b):
    return jnp.dot(a, b, preferred_element_type=F32)


def _dot_nt(a, b):
    return lax.dot_general(a, b, (((1,), (1,)), ((), ())), preferred_element_type=F32)


def _resident(shape):
    return pl.BlockSpec(shape, lambda *_: (0,) * len(shape), pipeline_mode=pl.Buffered(1))


def _params(*semantics):
    return pltpu.CompilerParams(dimension_semantics=semantics, vmem_limit_bytes=VMEM_LIMIT)


def _ffn_steps(x_ref, gpre_ref, weights, gpost_ref, o_ref, xn_ref, active=True):
    j = pl.program_id(1)
    last = pl.num_programs(1) - 1
    tm = x_ref.shape[0]
    pieces = [pl.ds(i * (tm // FFN_SPLIT), tm // FFN_SPLIT) for i in range(FFN_SPLIT)]

    def prenorm(rows):
        for c in range(rows.size // NORM_ROWS):
            r = pl.ds(rows.start + c * NORM_ROWS, NORM_ROWS)
            xn_ref[r, :] = _rms(x_ref[r, :], gpre_ref[...]).astype(BF16)

    def postnorm(rows):
        for c in range(rows.size // NORM_ROWS):
            r = pl.ds(rows.start + c * NORM_ROWS, NORM_ROWS)
            o_ref[r, :] = x_ref[r, :] + 0.5 * _rms(o_ref[r, :], gpost_ref[...])

    def swiglu(w, rows, accumulate):
        wg, wu, wd = w
        xn = xn_ref[rows, :]
        gate = _dot(xn, wg)
        up = _dot(xn, wu)
        down = _dot((gate * jax.nn.sigmoid(gate) * up).astype(BF16), wd)
        o_ref[rows, :] = o_ref[rows, :] + down if accumulate else down

    @pl.when((j == 0) & active)
    def _():
        w = weights()
        for rows in pieces:
            prenorm(rows)
        for rows in pieces:
            swiglu(w, rows, False)

    @pl.when((j > 0) & (j < last) & active)
    def _():
        swiglu(weights(), pl.ds(0, tm), True)

    @pl.when((j == last) & active)
    def _():
        w = weights()
        for rows in pieces:
            swiglu(w, rows, True)
        for rows in pieces:
            postnorm(rows)


def _ffn_head_kernel(x_ref, gpre_ref, wg32_hbm, wu32_hbm, wd32_hbm, gpost_ref,
                     o_ref, wg_ref, wu_ref, wd_ref, xn_ref, g_buf, u_buf, d_buf, sem):
    j = pl.program_id(1)
    n_steps = pl.num_programs(1)
    tf = g_buf.shape[2]

    def chunk_copies(c):
        slot = c % FFN_HEAD_BUFFERS
        cols = pl.ds(pl.multiple_of(c * tf, tf), tf)
        return (pltpu.make_async_copy(wg32_hbm.at[:, cols], g_buf.at[slot], sem.at[slot, 0]),
                pltpu.make_async_copy(wu32_hbm.at[:, cols], u_buf.at[slot], sem.at[slot, 1]),
                pltpu.make_async_copy(wd32_hbm.at[cols, :], d_buf.at[slot], sem.at[slot, 2]))

    @pl.when(j == 0)
    def _():
        for c in range(FFN_HEAD_BUFFERS - 1):
            for copy in chunk_copies(c):
                copy.start()

    @pl.when(j + FFN_HEAD_BUFFERS - 1 < n_steps)
    def _():
        for copy in chunk_copies(j + FFN_HEAD_BUFFERS - 1):
            copy.start()

    for copy in chunk_copies(j):
        copy.wait()

    def weights():
        slot = j % FFN_HEAD_BUFFERS
        w = [buf[slot].astype(BF16) for buf in (g_buf, u_buf, d_buf)]
        for dst, val in zip((wg_ref, wu_ref, wd_ref), w):
            dst[...] = val
        return w

    _ffn_steps(x_ref, gpre_ref, weights, gpost_ref, o_ref, xn_ref)


def _ffn_main_kernel(x_ref, gpre_ref, wg_ref, wu_ref, wd_ref, gpost_ref, o_ref, xn_ref):
    i = pl.program_id(0)

    @pl.when((i == 0) & (pl.program_id(1) == 0))
    def _():
        o_ref[...] = jnp.zeros_like(o_ref)

    _ffn_steps(x_ref, gpre_ref, lambda: (wg_ref[...], wu_ref[...], wd_ref[...]), gpost_ref, o_ref, xn_ref,
               active=i > 0)


def _ffn(h, g_pre, w_gate, w_up, w_down, g_post):
    t, d = h.shape
    d_ff = w_gate.shape[1]

    tm, tf = FFN_TM, FFN_HEAD_TF
    head_out, wg, wu, wd = pl.pallas_call(
        _ffn_head_kernel,
        grid=(1, d_ff // tf),
        in_specs=[
            pl.BlockSpec((tm, d), lambda i, j: (0, 0), pipeline_mode=pl.Buffered(1)),
            _resident((1, d)),
            pl.BlockSpec(memory_space=pl.ANY),
            pl.BlockSpec(memory_space=pl.ANY),
            pl.BlockSpec(memory_space=pl.ANY),
            _resident((1, d)),
        ],
        out_specs=[
            pl.BlockSpec((tm, d), lambda i, j: (0, 0)),
            pl.BlockSpec((d, tf), lambda i, j: (0, j)),
            pl.BlockSpec((d, tf), lambda i, j: (0, j)),
            pl.BlockSpec((tf, d), lambda i, j: (j, 0)),
        ],
        out_shape=[jax.ShapeDtypeStruct((tm, d), F32), jax.ShapeDtypeStruct(w_gate.shape, BF16),
                   jax.ShapeDtypeStruct(w_up.shape, BF16), jax.ShapeDtypeStruct(w_down.shape, BF16)],
        scratch_shapes=[pltpu.VMEM((tm, d), BF16),
                        pltpu.VMEM((FFN_HEAD_BUFFERS, d, tf), F32), pltpu.VMEM((FFN_HEAD_BUFFERS, d, tf), F32),
                        pltpu.VMEM((FFN_HEAD_BUFFERS, tf, d), F32),
                        pltpu.SemaphoreType.DMA((FFN_HEAD_BUFFERS, 3))],
        compiler_params=_params("arbitrary", "arbitrary"),
        name="ffn_head",
    )(h, g_pre, w_gate, w_up, w_down, g_post)

    tf = FFN_TF

    def step(i, j):
        return jnp.where(i == 0, 0, j)

    rest = pl.pallas_call(
        _ffn_main_kernel,
        grid=(t // tm, d_ff // tf),
        in_specs=[
            pl.BlockSpec((tm, d), lambda i, j: (i, 0)),
            _resident((1, d)),
            pl.BlockSpec((d, tf), lambda i, j: (0, step(i, j))),
            pl.BlockSpec((d, tf), lambda i, j: (0, step(i, j))),
            pl.BlockSpec((tf, d), lambda i, j: (step(i, j), 0)),
            _resident((1, d)),
        ],
        out_specs=pl.BlockSpec((tm, d), lambda i, j: (i, 0)),
        out_shape=jax.ShapeDtypeStruct((t, d), F32),
        scratch_shapes=[pltpu.VMEM((tm, d), BF16)],
        compiler_params=_params("parallel", "arbitrary"),
        name="ffn",
    )(h, g_pre, wg, wu, wd, g_post)
    return lax.dynamic_update_slice(rest, head_out, (0, 0))


def _qkv_kernel(h_ref, g_ref, w_ref, b_ref, nat_ref, a4_ref, a16_ref, zb_ref, zs_ref, z4_ref):
    tm = h_ref.shape[0]
    nc = zs_ref.shape[0]
    per_chunk = MXU_WIDTH // LANES
    xn = _rms(h_ref[...], g_ref[...]).astype(BF16)
    zb_ref[...] = (_dot(xn, w_ref[:, nc * LANES:]) + b_ref[:, nc * LANES:]).astype(BF16)
    for c2 in range(nc // per_chunk):
        wide = slice(c2 * MXU_WIDTH, (c2 + 1) * MXU_WIDTH)
        z2 = _dot(xn, w_ref[:, wide]) + b_ref[:, wide]
        for half in range(per_chunk):
            c = c2 * per_chunk + half
            zc = z2[:, half * LANES:(half + 1) * LANES]
            zs_ref[c] = zc
            nat_ref[c] = zc.astype(BF16)

    def copy(c, carry):
        for r in range(4):
            z4 = zs_ref[c, pl.ds(r, tm // 4, stride=4), :]
            a4_ref[0, c, r] = z4.astype(BF16)
            z4_ref[r] = z4
        for r in range(16):
            a16_ref[0, c, r] = z4_ref[r % 4, pl.ds(r // 4, tm // 16, stride=4), :].astype(BF16)
        return carry

    lax.fori_loop(0, nc, copy, 0)


def _qkv(h, g, w, b, bsz, seq):
    t, d = h.shape
    tm = QKV_TM
    nc = 3 * A_WIDTH // LANES
    per_seq = seq // tm
    return pl.pallas_call(
        _qkv_kernel,
        grid=(t // tm,),
        in_specs=[
            pl.BlockSpec((tm, d), lambda i: (i, 0)),
            _resident((1, d)),
            _resident(w.shape),
            _resident(b.shape),
        ],
        out_specs=[
            pl.BlockSpec((nc, tm, LANES), lambda i: (0, i, 0)),
            pl.BlockSpec((1, nc, 4, tm // 4, LANES), lambda i: (i // per_seq, 0, 0, i % per_seq, 0)),
            pl.BlockSpec((1, nc, 16, tm // 16, LANES), lambda i: (i // per_seq, 0, 0, i % per_seq, 0)),
            pl.BlockSpec((tm, ZB_WIDTH), lambda i: (i, 0)),
        ],
        out_shape=[
            jax.ShapeDtypeStruct((nc, t, LANES), BF16),
            jax.ShapeDtypeStruct((bsz, nc, 4, seq // 4, LANES), BF16),
            jax.ShapeDtypeStruct((bsz, nc, 16, seq // 16, LANES), BF16),
            jax.ShapeDtypeStruct((t, ZB_WIDTH), BF16),
        ],
        scratch_shapes=[pltpu.VMEM((nc, tm, LANES), F32), pltpu.VMEM((4, tm // 4, LANES), F32)],
        compiler_params=_params("parallel"),
        name="qkv",
    )(h, g, w, b)


def _attn_a_kernel(slopes_ref,
                   q1_ref, k1_ref, v1_ref, q4_ref, k4_ref, v4_ref, q16_ref, k16_ref, v16_ref,
                   o_ref, acc_ref, m_ref, l_ref, bias_ref, bias16_ref):
    hp = pl.program_id(1)
    seq = q1_ref.shape[1]
    lane = lax.broadcasted_iota(jnp.int32, (1, LANES), 1)
    head0 = lane < HEAD_DIM
    neg_slopes = (-slopes_ref[4 * hp], -slopes_ref[4 * hp + 2])

    def bias_table(nk, lead, dil):
        qi = lax.broadcasted_iota(jnp.int32, (BLOCK, nk), 0)
        kj = lax.broadcasted_iota(jnp.int32, (BLOCK, nk), 1)
        rel = qi - kj if lead else qi + (nk - BLOCK) - kj
        valid = (rel >= 0) & (rel <= A_WINDOW)
        rel_f = rel.astype(F32)
        return jnp.concatenate(
            [jnp.where(valid, rel_f * (ns * float(dil)), NEG_INF) for ns in neg_slopes], axis=0)

    for bi, dil in enumerate(DILATIONS[:2]):
        bias_ref[bi, 0] = bias_table(2 * BLOCK, False, dil)
        bias_ref[bi, 1] = bias_table(2 * BLOCK, True, dil)
    bias16_ref[...] = bias_table(BLOCK, True, DILATIONS[2])

    def unit(q, k, v, bias):
        zero = jnp.zeros_like(q)
        qs = jnp.concatenate([jnp.where(head0, q, zero), jnp.where(head0, zero, q)], axis=0)
        s = _dot_nt(qs * QK_SCALE, k) + bias
        m = jnp.max(s, axis=-1, keepdims=True)
        e = jnp.exp(s - m)
        l = jnp.sum(e, axis=-1, keepdims=True)
        pv = _dot(e.astype(BF16), v)
        return (jnp.where(head0, pv[:BLOCK], pv[BLOCK:]),
                jnp.where(head0, m[:BLOCK], m[BLOCK:]),
                jnp.where(head0, l[:BLOCK], l[BLOCK:]))

    def store(branch, rows, res):
        acc_ref[branch, rows, :] = res[0]
        m_ref[branch, rows, :] = res[1]
        l_ref[branch, rows, :] = res[2]

    unroll = ATTN_A_UNITS_PER_ITER
    nb4 = seq // 4 // BLOCK

    def body1(g, carry):
        for i in range(unroll):
            blk = g * unroll + i
            q0 = pl.multiple_of(blk * BLOCK, BLOCK)
            if i == 0:
                k0 = pl.multiple_of(jnp.maximum(blk - 1, 0) * BLOCK, BLOCK)
                bias = bias_ref[0, jnp.where(blk == 0, 1, 0)]
            else:
                k0 = pl.multiple_of((blk - 1) * BLOCK, BLOCK)
                bias = bias_ref[0, 0]
            store(0, pl.ds(q0, BLOCK),
                  unit(q1_ref[0, pl.ds(q0, BLOCK), :], k1_ref[0, pl.ds(k0, 2 * BLOCK), :],
                       v1_ref[0, pl.ds(k0, 2 * BLOCK), :], bias))
        return carry

    lax.fori_loop(0, seq // BLOCK // unroll, body1, 0)

    def body4(g, carry):
        for i in range(unroll):
            r = g * (unroll // nb4) + i // nb4
            blk = i % nb4
            k0 = max(blk - 1, 0) * BLOCK
            store(1, pl.ds(r + 4 * blk * BLOCK, BLOCK, stride=4),
                  unit(q4_ref[0, r, pl.ds(blk * BLOCK, BLOCK), :],
                       k4_ref[0, r, pl.ds(k0, 2 * BLOCK), :],
                       v4_ref[0, r, pl.ds(k0, 2 * BLOCK), :], bias_ref[1, 1 if blk == 0 else 0]))
        return carry

    lax.fori_loop(0, 4 * nb4 // unroll, body4, 0)

    def body16(g, carry):
        for i in range(unroll):
            r = g * unroll + i
            store(2, pl.ds(r, BLOCK, stride=16),
                  unit(q16_ref[0, r], k16_ref[0, r], v16_ref[0, r], bias16_ref[...]))
        return carry

    lax.fori_loop(0, 16 // unroll, body16, 0)

    def merge(blk, carry):
        rows = pl.ds(pl.multiple_of(blk * (2 * BLOCK), 2 * BLOCK), 2 * BLOCK)
        m0, m1, m2 = m_ref[0, rows, :], m_ref[1, rows, :], m_ref[2, rows, :]
        mx = jnp.maximum(jnp.maximum(m0, m1), m2)
        w0, w1, w2 = jnp.exp(m0 - mx), jnp.exp(m1 - mx), jnp.exp(m2 - mx)
        num = w0 * acc_ref[0, rows, :] + w1 * acc_ref[1, rows, :] + w2 * acc_ref[2, rows, :]
        den = w0 * l_ref[0, rows, :] + w1 * l_ref[1, rows, :] + w2 * l_ref[2, rows, :]
        o_ref[0, rows, :] = num / den
        return carry

    lax.fori_loop(0, seq // (2 * BLOCK), merge, 0)


def _attn_a(slopes, nat, a4, a16):
    _, bsz, seq, _ = nat.shape
    pairs = A_WIDTH // LANES

    def col(base):
        return lambda b, hp: (base + hp, b, 0, 0)

    def col4(base):
        return lambda b, hp: (b, base + hp, 0, 0, 0)

    nat_spec = [pl.BlockSpec((None, 1, seq, LANES), col(i * pairs)) for i in range(3)]
    a4_spec = [pl.BlockSpec((1, None, 4, seq // 4, LANES), col4(i * pairs)) for i in range(3)]
    a16_spec = [pl.BlockSpec((1, None, 16, seq // 16, LANES), col4(i * pairs)) for i in range(3)]
    return pl.pallas_call(
        _attn_a_kernel,
        grid=(bsz, pairs),
        in_specs=[pl.BlockSpec(memory_space=pltpu.SMEM)] + nat_spec + a4_spec + a16_spec,
        out_specs=pl.BlockSpec((1, seq, LANES), lambda b, hp: (b, 0, hp)),
        out_shape=jax.ShapeDtypeStruct((bsz, seq, A_WIDTH), F32),
        scratch_shapes=[pltpu.VMEM((3, seq, LANES), F32)] * 3 + [
            pltpu.VMEM((2, 2, 2 * BLOCK, 2 * BLOCK), F32), pltpu.VMEM((2 * BLOCK, BLOCK), F32)],
        compiler_params=_params("parallel", "parallel"),
        name="attn_a",
    )(slopes, nat, nat, nat, a4, a4, a4, a16, a16, a16)


def _attn_b_kernel(slopes_ref, sinks_ref, q_ref, k_ref, v_ref, o_ref, kd_ref, vd_ref, bias_ref):
    g = pl.program_id(1)
    seq = q_ref.shape[1]
    lane = lax.broadcasted_iota(jnp.int32, (1, LANES), 1)
    head0 = lane < HEAD_DIM
    mine = (lane >= HEAD_DIM).astype(jnp.int32) == g

    def both_halves(ref):
        x = ref[0].astype(F32)
        return jnp.where(mine, x, pltpu.roll(x, HEAD_DIM, 1)).astype(BF16)

    kd_ref[...] = both_halves(k_ref)
    vd_ref[...] = both_halves(v_ref)

    neg_slopes = [-slopes_ref[2 * (g * B_GROUP + j) + 1] for j in range(B_GROUP)]
    sinks = [sinks_ref[g * B_GROUP + j] for j in range(B_GROUP)]

    for lead in (False, True):
        qi = lax.broadcasted_iota(jnp.int32, (BLOCK, 2 * BLOCK), 0)
        kj = lax.broadcasted_iota(jnp.int32, (BLOCK, 2 * BLOCK), 1)
        rel = qi - kj if lead else qi + BLOCK - kj
        valid = (rel >= 0) & (rel <= B_WINDOW)
        rel_f = rel.astype(F32)
        for j in range(B_GROUP):
            bias_ref[int(lead), pl.ds(j * BLOCK, BLOCK), :] = jnp.where(valid, rel_f * neg_slopes[j], NEG_INF)

    def unit(blk, lead_possible):
        q0 = pl.multiple_of(blk * BLOCK, BLOCK)
        if lead_possible:
            k0 = pl.multiple_of(jnp.maximum(blk - 1, 0) * BLOCK, BLOCK)
            bias = bias_ref[jnp.where(blk == 0, 1, 0)]
        else:
            k0 = pl.multiple_of((blk - 1) * BLOCK, BLOCK)
            bias = bias_ref[0]
        k = kd_ref[pl.ds(k0, 2 * BLOCK), :]
        v = vd_ref[pl.ds(k0, 2 * BLOCK), :]
        q = q_ref[0, pl.ds(q0, BLOCK), :] * QK_SCALE
        stacked = []
        for j in range(B_GROUP):
            qp = q[:, (j // 2) * LANES:(j // 2 + 1) * LANES]
            stacked.append(jnp.where(head0 if j % 2 == 0 else ~head0, qp, jnp.zeros_like(qp)))
        s_all = _dot_nt(jnp.concatenate(stacked, axis=0), k) + bias
        probs, inv = [], []
        for j in range(B_GROUP):
            s = s_all[j * BLOCK:(j + 1) * BLOCK]
            m = jnp.maximum(jnp.max(s, axis=-1, keepdims=True), sinks[j])
            e = jnp.exp(s - m)
            inv.append(1.0 / (jnp.sum(e, axis=-1, keepdims=True) + jnp.exp(sinks[j] - m)))
            probs.append(e.astype(BF16))
        pv = _dot(jnp.concatenate(probs, axis=0), v)
        for p in range(B_GROUP // 2):
            even = pv[(2 * p) * BLOCK:(2 * p + 1) * BLOCK]
            odd = pv[(2 * p + 1) * BLOCK:(2 * p + 2) * BLOCK]
            o_ref[0, pl.ds(q0, BLOCK), p * LANES:(p + 1) * LANES] = (
                jnp.where(head0, even, odd) * jnp.where(head0, inv[2 * p], inv[2 * p + 1]))

    def body(it, carry):
        for i in range(ATTN_B_UNITS_PER_ITER):
            unit(it * ATTN_B_UNITS_PER_ITER + i, i == 0)
        return carry

    lax.fori_loop(0, seq // BLOCK // ATTN_B_UNITS_PER_ITER, body, 0)


def _attn_b(slopes, sinks, zb):
    bsz, seq, _ = zb.shape
    qw = B_GROUP * HEAD_DIM
    k_blk = B_WIDTH // LANES
    return pl.pallas_call(
        _attn_b_kernel,
        grid=(bsz, B_KV_HEADS),
        in_specs=[
            pl.BlockSpec(memory_space=pltpu.SMEM),
            pl.BlockSpec(memory_space=pltpu.SMEM),
            pl.BlockSpec((1, seq, qw), lambda b, g: (b, 0, g)),
            pl.BlockSpec((1, seq, LANES), lambda b, g: (b, 0, k_blk)),
            pl.BlockSpec((1, seq, LANES), lambda b, g: (b, 0, k_blk + 1)),
        ],
        out_specs=pl.BlockSpec((1, seq, qw), lambda b, g: (b, 0, g)),
        out_shape=jax.ShapeDtypeStruct((bsz, seq, B_WIDTH), F32),
        scratch_shapes=[pltpu.VMEM((seq, LANES), BF16)] * 2 + [
            pltpu.VMEM((2, B_GROUP * BLOCK, 2 * BLOCK), F32)],
        compiler_params=_params("parallel", "parallel"),
        name="attn_b",
    )(slopes, sinks, zb, zb, zb)


def _out_kernel(h_ref, oa_ref, ob_ref, ga_ref, gb_ref, wo32_ref, bo_ref, gpost_ref, o_ref, wo_ref):
    @pl.when(pl.program_id(0) == 0)
    def _():
        wo_ref[...] = wo32_ref[...].astype(BF16)

    na = _rms(oa_ref[...], ga_ref[...]).astype(BF16)
    nb = _rms(ob_ref[...], gb_ref[...]).astype(BF16)
    o = _dot(na, wo_ref[0:A_WIDTH, :]) + _dot(nb, wo_ref[A_WIDTH:A_WIDTH + B_WIDTH, :]) + bo_ref[...]
    o_ref[...] = h_ref[...] + _rms(o, gpost_ref[...])


def _out_proj(h, oa, ob, g_a, g_b, w_o, b_o, g_post):
    t, d = h.shape
    tm = OUT_TM
    row = lambda i: (i, 0)
    return pl.pallas_call(
        _out_kernel,
        grid=(t // tm,),
        in_specs=[
            pl.BlockSpec((tm, d), row),
            pl.BlockSpec((tm, A_WIDTH), row),
            pl.BlockSpec((tm, B_WIDTH), row),
            _resident((1, A_WIDTH)),
            _resident((1, B_WIDTH)),
            _resident((A_WIDTH + B_WIDTH, d)),
            _resident((1, d)),
            _resident((1, d)),
        ],
        out_specs=pl.BlockSpec((tm, d), row),
        out_shape=jax.ShapeDtypeStruct((t, d), F32),
        scratch_shapes=[pltpu.VMEM(w_o.shape, BF16)],
        compiler_params=_params("arbitrary"),
        name="out_proj",
    )(h, oa, ob, g_a, g_b, w_o, b_o, g_post)


def _ple_kernel(h_ref, p_ref, gpre_ref, wg_ref, wp_ref, gpost_ref, o_ref):
    h = h_ref[...]
    gate = jax.nn.sigmoid(_dot(_rms(h, gpre_ref[...]).astype(BF16), wg_ref[...]))
    e = _dot(p_ref[...].astype(BF16), wp_ref[...])
    o_ref[...] = h + _rms(gate * e, gpost_ref[...])


def _ple(h, p, g_pre, w_gate, w_proj, g_post):
    t, d = h.shape
    pd = p.shape[1]
    tm = PLE_TM
    row = lambda i: (i, 0)
    return pl.pallas_call(
        _ple_kernel,
        grid=(t // tm,),
        in_specs=[
            pl.BlockSpec((tm, d), row),
            pl.BlockSpec((tm, pd), row),
            _resident((1, d)),
            _resident((d, d)),
            _resident((pd, d)),
            _resident((1, d)),
        ],
        out_specs=pl.BlockSpec((tm, d), row),
        out_shape=jax.ShapeDtypeStruct((t, d), F32),
        compiler_params=_params("parallel"),
        name="ple",
    )(h, p, g_pre, w_gate, w_proj, g_post)


def _alibi_slopes(n):
    return jnp.exp2(-ALIBI_MAX_BIAS * (jnp.arange(n, dtype=F32) + 1.0) / n)


def kernel(x, p, g_ffn1_pre, w_ffn1_gate, w_ffn1_up, w_ffn1_down, g_ffn1_post, g_mix_pre, w_qkv, b_qkv, attn_sinks, g_out_a, g_out_b, w_o, b_o, g_mix_post, g_ffn2_pre, w_ffn2_gate, w_ffn2_up, w_ffn2_down, g_ffn2_post, g_ple_pre, w_ple_gate, w_ple_proj, g_ple_post):
    bsz, seq, d = x.shape
    depth = p.shape[0]
    t = bsz * seq
    slopes = _alibi_slopes(A_HEADS + B_Q_HEADS)
    bf = lambda w: w.astype(BF16)
    h = x.reshape(t, d)
    for i in range(depth):
        h = _ffn(h, g_ffn1_pre[i][None], w_ffn1_gate[i], w_ffn1_up[i], w_ffn1_down[i], g_ffn1_post[i][None])
        nat, a4, a16, zb = _qkv(h, g_mix_pre[i][None], bf(w_qkv[i]), b_qkv[i][None], bsz, seq)
        oa = _attn_a(slopes, nat.reshape(-1, bsz, seq, LANES), a4, a16)
        ob = _attn_b(slopes, attn_sinks[i].reshape(-1), zb.reshape(bsz, seq, ZB_WIDTH))
        h = _out_proj(h, oa.reshape(t, A_WIDTH), ob.reshape(t, B_WIDTH), g_out_a[i][None], g_out_b[i][None],
                      w_o[i], b_o[i][None], g_mix_post[i][None])
        h = _ffn(h, g_ffn2_pre[i][None], w_ffn2_gate[i], w_ffn2_up[i], w_ffn2_down[i], g_ffn2_post[i][None])
        h = _ple(h, p[i].reshape(t, -1), g_ple_pre[i][None], bf(w_ple_gate[i]), bf(w_ple_proj[i]),
                 g_ple_post[i][None])
    return h.reshape(bsz, seq, d)
```
